```python
import math
import jax, jax.numpy as jnp
from jax import lax
import numpy as np

D_MODEL = 2048
BATCH = 4
SEQ = 4096
DEPTH = 2

F32 = jnp.float32

ALPHA = (2.0 * DEPTH) ** 0.25
BETA = (8.0 * DEPTH) ** -0.25

RWKV_HEAD_DIM = 64
RWKV_DIM = D_MODEL // 2
RWKV_HEADS = RWKV_DIM // RWKV_HEAD_DIM
DECAY_LORA = 96
ICL_LORA = 96
GATE_LORA = 256
W_DECAY_SCALE = 0.606531
RWKV_GN_EPS = 64e-5
RWKV_SHIFT_COLS = 3 * RWKV_DIM + DECAY_LORA + ICL_LORA + GATE_LORA

GLA_HEADS = 4
GLA_VDIM = D_MODEL // 2
GLA_KDIM = GLA_VDIM // 2
GLA_DK = GLA_KDIM // GLA_HEADS
GLA_DV = GLA_VDIM // GLA_HEADS
GLA_GATE_LORA = 16
GLA_GATE_NORMALIZER = 16.0
GLA_CHUNK = 64
AB_IN = RWKV_SHIFT_COLS + 2 * GLA_KDIM + GLA_VDIM + GLA_GATE_LORA + GLA_VDIM
AB_OUT = RWKV_DIM + GLA_VDIM

DIFF_HEAD_DIM = 64
DIFF_V_DIM = 2 * DIFF_HEAD_DIM
DIFF_HEADS = D_MODEL // DIFF_V_DIM
DIFF_Q_DIM = DIFF_HEADS * 2 * DIFF_HEAD_DIM
DIFF_VDIM_TOT = DIFF_HEADS * DIFF_V_DIM
Q_BLOCK = 128
ROPE_THETA = 10000.0

D_FF = -(-(8 * D_MODEL) // (3 * 256)) * 256

N_EVEN = (DEPTH + 1) // 2
N_ODD = DEPTH // 2

kernel_name = "bidir_hybrid_rwkv7_gla_diffattn_deepnorm_adaln"


def _split(z, sizes):
    idx = np.cumsum(sizes)[:-1].tolist()
    return jnp.split(z, idx, axis=-1)


def _layer_norm(x, g, b, eps=1e-5):
    xf = x.astype(F32)
    mu = jnp.mean(xf, -1, keepdims=True)
    var = jnp.mean(jnp.square(xf - mu), -1, keepdims=True)
    return ((xf - mu) * lax.rsqrt(var + eps)).astype(x.dtype) * g + b


def _rms_norm(x, g, eps=1e-5):
    xf = x.astype(F32)
    return (xf * lax.rsqrt(jnp.mean(xf * xf, -1, keepdims=True) + eps)).astype(x.dtype) * g


def _l2norm(t, eps=1e-12):
    tf = t.astype(F32)
    return (tf / jnp.maximum(jnp.sqrt(jnp.sum(tf * tf, -1, keepdims=True)), eps)).astype(t.dtype)


def _centred_shift(z):
    zp = jnp.pad(z, ((0, 0), (1, 1), (0, 0)))
    return 0.5 * (zp[:, :-2] + zp[:, 2:])


def _ada(c, w, b):
    mod = jax.nn.silu(c) @ w + b
    shift, scale, gate = jnp.split(mod[:, None, :], 3, axis=-1)
    return shift, scale, gate


def _rwkv7_scan(r, w, k, v, kk, a):
    dt = r.dtype
    xs = tuple(t.astype(F32) for t in (r, w, k, v, kk, a))

    def step(S, inp):
        r_t, w_t, k_t, v_t, kk_t, a_t = inp
        sa = -jnp.einsum('dbhvk,dbhk->dbhv', S, kk_t)
        S = (S * w_t[..., None, :] + sa[..., :, None] * (kk_t * a_t)[..., None, :]
             + v_t[..., :, None] * k_t[..., None, :])
        return S, jnp.einsum('dbhvk,dbhk->dbhv', S, r_t)

    S0 = jnp.zeros(r.shape[1:] + (r.shape[-1],), F32)
    _, y = lax.scan(step, S0, xs)
    return y.astype(dt)


def _rwkv7_mixer(z, mu, w0, w_up, a0, a_up, g_up, k_k, k_a, r_k, gn_g, gn_b):
    B, T, _ = z.shape
    H, N = RWKV_HEADS, RWKV_HEAD_DIM
    z = z + (_centred_shift(z) - z) * mu
    r, k, v, wd, ad, gd = _split(z, (RWKV_DIM, RWKV_DIM, RWKV_DIM, DECAY_LORA, ICL_LORA, GATE_LORA))
    w = jnp.exp(-W_DECAY_SCALE * jax.nn.sigmoid(
        w0[:, None, None, :] + jnp.einsum('btr,drc->dbtc', jnp.tanh(wd), w_up)))
    a = jax.nn.sigmoid(a0 + ad @ a_up)
    g = jax.nn.sigmoid(gd) @ g_up
    heads = lambda t: t.reshape(t.shape[:-1] + (H, N))
    kk = _l2norm(heads(k * k_k))
    k = k * (1.0 + (a - 1.0) * k_a)
    r_h, k_h, v_h, a_h, w_h = heads(r), heads(k), heads(v), heads(a), heads(w)
    tm = lambda t: jnp.moveaxis(t, 1, 0)

    def both(t):
        t = tm(t)
        return jnp.stack([t, t[::-1]], axis=1)

    w_t = jnp.stack([tm(w_h[0]), tm(w_h[1])[::-1]], axis=1)
    y = _rwkv7_scan(both(r_h), w_t, both(k_h), both(v_h), both(kk), both(a_h))
    y = jnp.moveaxis(y[:, 0] + y[::-1, 1], 0, 1)
    y = _layer_norm(y, gn_g.reshape(H, N), gn_b.reshape(H, N), eps=RWKV_GN_EPS)
    y = y + jnp.sum(r_h * k_h * r_k, -1, keepdims=True) * v_h
    return y.reshape(B, T, RWKV_DIM) * g


def _gla_chunked(q, k, v, log_g):
    dt = v.dtype
    q, k, v, log_g = (t.astype(F32) for t in (q, k, v, log_g))
    T = q.shape[-2]
    n = T // GLA_CHUNK
    chunk = lambda t: t.reshape(t.shape[:-2] + (n, GLA_CHUNK, t.shape[-1]))
    q, k, v, log_g = chunk(q), chunk(k), chunk(v), chunk(log_g)
    b = lax.cumsum(log_g, axis=log_g.ndim - 2)
    b_last = b[..., -1:, :]
    qb = q * jnp.exp(b)
    kb = k * jnp.exp(-b)
    mask = jnp.tril(jnp.ones((GLA_CHUNK, GLA_CHUNK), bool))
    att = jnp.where(mask, jnp.einsum('...nid,...njd->...nij', qb, kb), 0.0)
    o = jnp.einsum('...nij,...njv->...niv', att, v)
    kv = jnp.einsum('...ncd,...ncv->...ndv', k * jnp.exp(b_last - b), v)
    decay = jnp.exp(b_last[..., 0, :])

    def step(S, inp):
        dec, kv_n = inp
        return S * dec[..., :, None] + kv_n, S

    S0 = jnp.zeros(kv.shape[:-3] + kv.shape[-2:], F32)
    _, S_prev = lax.scan(step, S0, (jnp.moveaxis(decay, -2, 0), jnp.moveaxis(kv, -3, 0)))
    S_prev = jnp.moveaxis(S_prev, 0, -3)
    o = o + jnp.einsum('...ncd,...ndv->...ncv', qb, S_prev)
    return o.reshape(o.shape[:-3] + (T, o.shape[-1])).astype(dt)


def _gla_mixer(q, k, v, gg, og, gate_up, gate_b, norm_g):
    B, T, _ = q.shape
    H = GLA_HEADS
    log_g = jax.nn.log_sigmoid(jnp.einsum('btr,drc->dbtc', gg, gate_up)
                               + gate_b[:, None, None, :]) / GLA_GATE_NORMALIZER

    def heads(t, dh):
        return jnp.swapaxes(t.reshape(t.shape[:-1] + (H, dh)), -3, -2)

    qh = heads(q, GLA_DK) * (GLA_DK ** -0.5)
    kh, vh, gh = heads(k, GLA_DK), heads(v, GLA_DV), heads(log_g, GLA_DK)
    flip = lambda t: t[..., ::-1, :]
    both = lambda t: jnp.stack([t, flip(t)])
    o = _gla_chunked(both(qh), both(kh), both(vh), jnp.stack([gh[0], flip(gh[1])]))
    o = o[0] + flip(o[1])
    o = _rms_norm(o, norm_g)
    o = jnp.swapaxes(o, 1, 2).reshape(B, T, GLA_VDIM)
    return o * jax.nn.silu(og)


def _mixer_ab(h, w_in, w_out, mu, w0, w_up, a0, a_up, g_up, k_k, k_a, r_k, gn_g, gn_b,
              gate_up, gate_b, norm_g):
    z = h @ w_in
    z_a, q, k, v, gg, og = _split(z, (RWKV_SHIFT_COLS, GLA_KDIM, GLA_KDIM, GLA_VDIM, GLA_GATE_LORA, GLA_VDIM))
    y_a = _rwkv7_mixer(z_a, mu, w0, w_up, a0, a_up, g_up, k_k, k_a, r_k, gn_g, gn_b)
    y_b = _gla_mixer(q, k, v, gg, og, gate_up, gate_b, norm_g)
    return jnp.concatenate([y_a, y_b], -1) @ w_out


def _rope_tables(T, d, dtype):
    inv = ROPE_THETA ** (-jnp.arange(0, d, 2, dtype=F32) / d)
    ang = jnp.arange(T, dtype=F32)[:, None] * inv[None, :]
    return jnp.cos(ang).astype(dtype), jnp.sin(ang).astype(dtype)


def _apply_rope(x, cos, sin):
    cos, sin = cos[:, None, None, :], sin[:, None, None, :]
    x1, x2 = jnp.split(x, 2, axis=-1)
    return jnp.concatenate([x1 * cos - x2 * sin, x2 * cos + x1 * sin], -1)


def _lambda_init(layer):
    return 0.8 - 0.6 * math.exp(-0.3 * layer)


def _mixer_c(h, w_in, w_out, lam_params, subln_g, lambda_init):
    B, T, _ = h.shape
    H, d = DIFF_HEADS, DIFF_HEAD_DIM
    q, k, v = _split(h @ w_in, (DIFF_Q_DIM, DIFF_Q_DIM, DIFF_VDIM_TOT))
    q = q.reshape(B, T, H, 2, d)
    k = k.reshape(B, T, H, 2, d)
    v = v.reshape(B, T, H, DIFF_V_DIM)
    cos, sin = _rope_tables(T, d, q.dtype)
    q, k = _apply_rope(q, cos, sin), _apply_rope(k, cos, sin)
    lp = lam_params.astype(F32)
    lam = jnp.exp(jnp.sum(lp[0] * lp[1])) - jnp.exp(jnp.sum(lp[2] * lp[3])) + lambda_init
    scale = d ** -0.5
    q_blocks = jnp.moveaxis(q.reshape(B, T // Q_BLOCK, Q_BLOCK, H, 2, d), 1, 0)

    def block(q_blk):
        s = jnp.einsum('bqhcd,bkhcd->bhcqk', q_blk, k, preferred_element_type=F32) * scale
        p = jax.nn.softmax(s, axis=-1)
        p = p[:, :, 0] - lam * p[:, :, 1]
        return jnp.einsum('bhqk,bkhv->bqhv', p.astype(v.dtype), v)

    o = lax.map(block, q_blocks)
    o = jnp.moveaxis(o, 0, 1).reshape(B, T, H, DIFF_V_DIM)
    o = _rms_norm(o, subln_g) * (1.0 - lambda_init)
    return o.reshape(B, T, DIFF_VDIM_TOT) @ w_out


def _swiglu(h, w_in, w_out):
    gate, up = jnp.split(h @ w_in, 2, axis=-1)
    return (jax.nn.silu(gate) * up) @ w_out


def setup_inputs(seed: int = 0) -> dict:
    key = jax.random.key(seed)
    ks = iter(jax.random.split(key, 40))
    nrm = lambda shape, s: jax.random.normal(next(ks), shape, F32) * s
    D = D_MODEL
    return {
        "x": nrm((BATCH, SEQ, D), 1.0),
        "c": nrm((BATCH, D), 1.0),
        "ada_w": nrm((DEPTH, 2, D, 3 * D), 0.2 * D ** -0.5),
        "ada_b": nrm((DEPTH, 2, 3 * D), 0.01),
        "ln_g": 1.0 + nrm((DEPTH, 2, D), 0.02),
        "ln_b": nrm((DEPTH, 2, D), 0.02),
        "ffn_w_in": nrm((DEPTH, D, 2 * D_FF), D ** -0.5),
        "ffn_w_out": nrm((DEPTH, D_FF, D), BETA * D_FF ** -0.5),
        "ab_w_in": nrm((N_EVEN, D, AB_IN), D ** -0.5),
        "ab_w_out": nrm((N_EVEN, AB_OUT, D), BETA * AB_OUT ** -0.5),
        "rwkv_mu": jax.random.uniform(next(ks), (N_EVEN, RWKV_SHIFT_COLS), F32),
        "rwkv_w0": jax.random.uniform(next(ks), (N_EVEN, 2, RWKV_DIM), F32, -4.0, 1.0),
        "rwkv_w_up": nrm((N_EVEN, 2, DECAY_LORA, RWKV_DIM), 0.1),
        "rwkv_a0": nrm((N_EVEN, RWKV_DIM), 0.5),
        "rwkv_a_up": nrm((N_EVEN, ICL_LORA, RWKV_DIM), ICL_LORA ** -0.5),
        "rwkv_g_up": nrm((N_EVEN, GATE_LORA, RWKV_DIM), GATE_LORA ** -0.5),
        "rwkv_k_k": 0.85 + nrm((N_EVEN, RWKV_DIM), 0.1),
        "rwkv_k_a": 1.0 + nrm((N_EVEN, RWKV_DIM), 0.1),
        "rwkv_r_k": nrm((N_EVEN, RWKV_HEADS, RWKV_HEAD_DIM), 0.1),
        "rwkv_gn_g": 1.0 + nrm((N_EVEN, RWKV_DIM), 0.02),
        "rwkv_gn_b": nrm((N_EVEN, RWKV_DIM), 0.02),
        "gla_gate_up": nrm((N_EVEN, 2, GLA_GATE_LORA, GLA_KDIM), GLA_GATE_LORA ** -0.5),
        "gla_gate_b": nrm((N_EVEN, 2, GLA_KDIM), 1.0),
        "gla_norm_g": 1.0 + nrm((N_EVEN, GLA_DV), 0.02),
        "diff_w_in": nrm((N_ODD, D, 2 * DIFF_Q_DIM + DIFF_VDIM_TOT), D ** -0.5),
        "diff_w_out": nrm((N_ODD, DIFF_VDIM_TOT, D), BETA * DIFF_VDIM_TOT ** -0.5),
        "diff_lambda": nrm((N_ODD, 4, DIFF_HEAD_DIM), 0.1),
        "diff_subln_g": 1.0 + nrm((N_ODD, DIFF_V_DIM), 0.02),
    }


def reference(x, c, ada_w, ada_b, ln_g, ln_b, ffn_w_in, ffn_w_out, ab_w_in, ab_w_out,
              rwkv_mu, rwkv_w0, rwkv_w_up, rwkv_a0, rwkv_a_up, rwkv_g_up, rwkv_k_k, rwkv_k_a,
              rwkv_r_k, rwkv_gn_g, rwkv_gn_b, gla_gate_up, gla_gate_b, gla_norm_g,
              diff_w_in, diff_w_out, diff_lambda, diff_subln_g):
    for i in range(DEPTH):
        j = i // 2
        shift, scale, gate = _ada(c, ada_w[i, 0], ada_b[i, 0])
        h = x * (1.0 + scale) + shift
        if i % 2 == 0:
            y = _mixer_ab(h, ab_w_in[j], ab_w_out[j], rwkv_mu[j], rwkv_w0[j], rwkv_w_up[j],
                          rwkv_a0[j], rwkv_a_up[j], rwkv_g_up[j], rwkv_k_k[j], rwkv_k_a[j],
                          rwkv_r_k[j], rwkv_gn_g[j], rwkv_gn_b[j],
                          gla_gate_up[j], gla_gate_b[j], gla_norm_g[j])
        else:
            y = _mixer_c(h, diff_w_in[j], diff_w_out[j], diff_lambda[j], diff_subln_g[j],
                         _lambda_init(i))
        x = _layer_norm(ALPHA * x + (1.0 + gate) * y, ln_g[i, 0], ln_b[i, 0])
        shift, scale, gate = _ada(c, ada_w[i, 1], ada_b[i, 1])
        h = x * (1.0 + scale) + shift
        y = _swiglu(h, ffn_w_in[i], ffn_w_out[i])
        x = _layer_norm(ALPHA * x + (1.0 + gate) * y, ln_g[i, 1], ln_b[i, 1])
    return x
```

```python
import functools
import math

import numpy as np
import jax
import jax.numpy as jnp
from jax import lax
from jax.experimental import pallas as pl
from jax.experimental.pallas import tpu as pltpu

F32 = jnp.float32
BF16 = jnp.bfloat16
HI = lax.Precision.HIGHEST

D_MODEL = 2048
DEPTH = 2
ALPHA = (2.0 * DEPTH) ** 0.25
LN_EPS = 1e-5

RWKV_HEAD_DIM = 64
RWKV_DIM = D_MODEL // 2
RWKV_HEADS = RWKV_DIM // RWKV_HEAD_DIM
DECAY_LORA = 96
ICL_LORA = 96
GATE_LORA = 256
W_DECAY_SCALE = 0.606531
RWKV_GN_EPS = 64e-5

GLA_HEADS = 4
GLA_VDIM = D_MODEL // 2
GLA_KDIM = GLA_VDIM // 2
GLA_DK = GLA_KDIM // GLA_HEADS
GLA_DV = GLA_VDIM // GLA_HEADS
GLA_GATE_LORA = 16
GLA_GATE_NORMALIZER = 16.0

DIFF_HEAD_DIM = 64
DIFF_V_DIM = 2 * DIFF_HEAD_DIM
DIFF_HEADS = D_MODEL // DIFF_V_DIM
ROPE_THETA = 10000.0

D_FF = -(-(8 * D_MODEL) // (3 * 256)) * 256

LANES = 128
VMEM_LIMIT = 56 * 1024 * 1024

CHUNK = 64
LORA_PAD = 128
Z_SEGS = {
    "rkv": (0, 3 * RWKV_DIM, 0),
    "wd": (3 * RWKV_DIM, DECAY_LORA, 3072),
    "ad": (3 * RWKV_DIM + DECAY_LORA, ICL_LORA, 3200),
    "gd": (3 * RWKV_DIM + DECAY_LORA + ICL_LORA, GATE_LORA, 3328),
    "q": (3520, GLA_KDIM, 3584),
    "k": (3520 + GLA_KDIM, GLA_KDIM, 4096),
    "v": (3520 + 2 * GLA_KDIM, GLA_VDIM, 4608),
    "gg": (3520 + 2 * GLA_KDIM + GLA_VDIM, GLA_GATE_LORA, 6656),
    "og": (3520 + 2 * GLA_KDIM + GLA_VDIM + GLA_GATE_LORA, GLA_VDIM, 5632),
}
Z_A_COLS = 3584
Z_COLS = 7168


def _cparams(sem):
    return pltpu.CompilerParams(dimension_semantics=sem, vmem_limit_bytes=VMEM_LIMIT)


def _dot(a, b, prec=None):
    return jnp.dot(a, b, preferred_element_type=F32, precision=prec)


def _dot_nt(a, b, prec=None):
    return lax.dot_general(a, b, (((1,), (1,)), ((), ())), preferred_element_type=F32,
                           precision=prec)


def _dot_tn(a, b, prec=None):
    return lax.dot_general(a, b, (((0,), (0,)), ((), ())), preferred_element_type=F32,
                           precision=prec)


def _sigmoid(x):
    return 1.0 / (1.0 + jnp.exp(-x))


def _layer_norm_rows(u, g, b, eps):
    mu = jnp.mean(u, axis=-1, keepdims=True)
    d = u - mu
    var = jnp.mean(d * d, axis=-1, keepdims=True)
    return d * lax.rsqrt(var + eps) * g + b


def _ada_body(c_ref, w_ref, b_ref, o_ref):
    c = c_ref[...]
    sc = (c * _sigmoid(c)).astype(BF16)
    o_ref[0] = _dot(sc, w_ref[0].astype(BF16)) + b_ref[0]


def ada_modulation(c, ada_w, ada_b):
    B, D = c.shape
    n = ada_w.shape[0] * ada_w.shape[1]
    w = ada_w.reshape(n, D, 3 * D)
    b = ada_b.reshape(n, 1, 3 * D)
    rows = 8
    c_pad = jnp.pad(c, ((0, rows - B), (0, 0)))
    tn = 1024
    out = pl.pallas_call(
        _ada_body,
        grid=(n, 3 * D // tn),
        in_specs=[
            pl.BlockSpec((rows, D), lambda i, j: (0, 0)),
            pl.BlockSpec((1, D, tn), lambda i, j: (i, 0, j)),
            pl.BlockSpec((1, 1, tn), lambda i, j: (i, 0, j)),
        ],
        out_specs=pl.BlockSpec((1, rows, tn), lambda i, j: (i, 0, j)),
        out_shape=jax.ShapeDtypeStruct((n, rows, 3 * D), F32),
        compiler_params=_cparams(("parallel", "parallel")),
        name="ada_modulation",
    )(c_pad, w, b)
    mods = out[:, :B, :]
    shift, scale, gate = mods[..., :D], mods[..., D:2 * D], mods[..., 2 * D:]
    r3 = lambda t: t.reshape(n, B, 1, D)
    return r3(shift), r3(scale), r3(gate)


def _inproj_body(x_ref, sh_ref, sc_ref, w_ref, o_ref, h_ref):
    @pl.when(pl.program_id(2) == 0)
    def _():
        h_ref[...] = (x_ref[0] * (1.0 + sc_ref[0]) + sh_ref[0]).astype(BF16)

    o_ref[0] = _dot(h_ref[...], w_ref[...]).astype(o_ref.dtype)


def modulated_projection(x, shift, scale, w, out_dtype, tm, tn):
    B, T, D = x.shape
    N = w.shape[1]
    return pl.pallas_call(
        _inproj_body,
        grid=(B, T // tm, N // tn),
        in_specs=[
            pl.BlockSpec((1, tm, D), lambda b, i, j: (b, i, 0)),
            pl.BlockSpec((1, 1, D), lambda b, i, j: (b, 0, 0)),
            pl.BlockSpec((1, 1, D), lambda b, i, j: (b, 0, 0)),
            pl.BlockSpec((D, tn), lambda b, i, j: (0, j)),
        ],
        out_specs=pl.BlockSpec((1, tm, tn), lambda b, i, j: (b, i, j)),
        out_shape=jax.ShapeDtypeStruct((B, T, N), out_dtype),
        scratch_shapes=[pltpu.VMEM((tm, D), BF16)],
        compiler_params=_cparams(("parallel", "parallel", "arbitrary")),
        name="modulated_projection",
    )(x, shift, scale, w)


def _qkv_body(x_ref, sh_ref, sc_ref, w_ref, cos_ref, sin_ref, o_ref, h_ref, *, n_q, n_qk,
              q_scale):
    j = pl.program_id(2)

    @pl.when(j == 0)
    def _():
        h_ref[...] = (x_ref[0] * (1.0 + sc_ref[0]) + sh_ref[0]).astype(BF16)

    acc = _dot(h_ref[...], w_ref[...])
    tn = acc.shape[1]

    @pl.when(j < n_qk)
    def _():
        cos = cos_ref[...]
        sin = sin_ref[...]
        mult = jnp.where(j < n_q, q_scale, 1.0).astype(F32)
        for s in range(tn // LANES):
            xs = acc[:, s * LANES:(s + 1) * LANES]
            rot = pltpu.roll(xs, LANES // 2, axis=1)
            o_ref[0, :, s * LANES:(s + 1) * LANES] = ((xs * cos + rot * sin) * mult).astype(o_ref.dtype)

    @pl.when(j >= n_qk)
    def _():
        o_ref[0] = acc.astype(o_ref.dtype)


def qkv_projection(x, shift, scale, w, cos_t, sin_t, tm, tn):
    B, T, D = x.shape
    N = w.shape[1]
    n_q = D_MODEL // tn
    body = functools.partial(_qkv_body, n_q=n_q, n_qk=2 * n_q, q_scale=DIFF_HEAD_DIM ** -0.5)
    return pl.pallas_call(
        body,
        grid=(B, T // tm, N // tn),
        in_specs=[
            pl.BlockSpec((1, tm, D), lambda b, i, j: (b, i, 0)),
            pl.BlockSpec((1, 1, D), lambda b, i, j: (b, 0, 0)),
            pl.BlockSpec((1, 1, D), lambda b, i, j: (b, 0, 0)),
            pl.BlockSpec((D, tn), lambda b, i, j: (0, j)),
            pl.BlockSpec((tm, LANES), lambda b, i, j: (i, 0)),
            pl.BlockSpec((tm, LANES), lambda b, i, j: (i, 0)),
        ],
        out_specs=pl.BlockSpec((1, tm, tn), lambda b, i, j: (b, i, j)),
        out_shape=jax.ShapeDtypeStruct((B, T, N), BF16),
        scratch_shapes=[pltpu.VMEM((tm, D), BF16)],
        compiler_params=_cparams(("parallel", "parallel", "arbitrary")),
        name="qkv_projection",
    )(x, shift, scale, w, cos_t, sin_t)


def _proj_ln_body(*refs, n_parts):
    y_refs = refs[:n_parts]
    w_ref, x_ref, gate_ref, g_ref, b_ref, o_ref = refs[n_parts:]
    acc = None
    off = 0
    for yr in y_refs:
        k = yr.shape[-1]
        part = _dot(yr[0], w_ref[off:off + k, :])
        acc = part if acc is None else acc + part
        off += k
    u = ALPHA * x_ref[0] + (1.0 + gate_ref[0]) * acc
    o_ref[0] = _layer_norm_rows(u, g_ref[...], b_ref[...], LN_EPS)


def projection_layernorm(y_parts, w, x, gate, ln_g, ln_b, tm):
    B, T, D = x.shape
    K = w.shape[0]
    in_specs = [pl.BlockSpec((1, tm, yp.shape[-1]), lambda b, i: (b, i, 0)) for yp in y_parts]
    in_specs += [
        pl.BlockSpec((K, D), lambda b, i: (0, 0)),
        pl.BlockSpec((1, tm, D), lambda b, i: (b, i, 0)),
        pl.BlockSpec((1, 1, D), lambda b, i: (b, 0, 0)),
        pl.BlockSpec((1, D), lambda b, i: (0, 0)),
        pl.BlockSpec((1, D), lambda b, i: (0, 0)),
    ]
    return pl.pallas_call(
        functools.partial(_proj_ln_body, n_parts=len(y_parts)),
        grid=(B, T // tm),
        in_specs=in_specs,
        out_specs=pl.BlockSpec((1, tm, D), lambda b, i: (b, i, 0)),
        out_shape=jax.ShapeDtypeStruct((B, T, D), F32),
        compiler_params=_cparams(("parallel", "parallel")),
        name="projection_layernorm",
    )(*y_parts, w, x, gate, ln_g.reshape(1, D), ln_b.reshape(1, D))


def _ffn_body(x_ref, sh_ref, sc_ref, gate_ref, wg_ref, wu_ref, wo_ref, g_ref, b_ref, o_ref,
              h_ref, acc_ref):
    j = pl.program_id(2)

    @pl.when(j == 0)
    def _():
        h_ref[...] = (x_ref[0] * (1.0 + sc_ref[0]) + sh_ref[0]).astype(BF16)
        acc_ref[...] = jnp.zeros_like(acc_ref)

    h = h_ref[...]
    gt = _dot(h, wg_ref[...])
    up = _dot(h, wu_ref[...])
    act = (gt * _sigmoid(gt) * up).astype(BF16)
    acc_ref[...] += _dot(act, wo_ref[...])

    @pl.when(j == pl.num_programs(2) - 1)
    def _():
        u = ALPHA * x_ref[0] + (1.0 + gate_ref[0]) * acc_ref[...]
        o_ref[0] = _layer_norm_rows(u, g_ref[...], b_ref[...], LN_EPS)


def ffn_sublayer(x, shift, scale, gate, w_in, w_out, ln_g, ln_b, tm, tf):
    B, T, D = x.shape
    F = w_out.shape[0]
    nf = F // tf
    return pl.pallas_call(
        _ffn_body,
        grid=(B, T // tm, nf),
        in_specs=[
            pl.BlockSpec((1, tm, D), lambda b, i, j: (b, i, 0)),
            pl.BlockSpec((1, 1, D), lambda b, i, j: (b, 0, 0)),
            pl.BlockSpec((1, 1, D), lambda b, i, j: (b, 0, 0)),
            pl.BlockSpec((1, 1, D), lambda b, i, j: (b, 0, 0)),
            pl.BlockSpec((D, tf), lambda b, i, j: (0, j)),
            pl.BlockSpec((D, tf), lambda b, i, j: (0, j + nf)),
            pl.BlockSpec((tf, D), lambda b, i, j: (j, 0)),
            pl.BlockSpec((1, D), lambda b, i, j: (0, 0)),
            pl.BlockSpec((1, D), lambda b, i, j: (0, 0)),
        ],
        out_specs=pl.BlockSpec((1, tm, D), lambda b, i, j: (b, i, 0)),
        out_shape=jax.ShapeDtypeStruct((B, T, D), F32),
        scratch_shapes=[pltpu.VMEM((tm, D), BF16), pltpu.VMEM((tm, D), F32)],
        compiler_params=_cparams(("parallel", "parallel", "arbitrary")),
        name="ffn_sublayer",
    )(x, shift, scale, gate, w_in, w_in, w_out, ln_g.reshape(1, D), ln_b.reshape(1, D))


def _attn_body(q_ref, k_ref, v_ref, lam_ref, g_ref, o_ref, *, lambda_init):
    q = q_ref[0]
    k = k_ref[0]
    v = v_ref[0]
    lane = lax.broadcasted_iota(jnp.int32, (1, LANES), 1)
    comp0 = (lane // (DIFF_HEAD_DIM // 2)) % 2 == 0
    zero = jnp.zeros_like(q)
    q0 = jnp.where(comp0, q, zero)
    q1 = jnp.where(comp0, zero, q)
    s0 = _dot_nt(q0, k)
    s1 = _dot_nt(q1, k)
    e0 = jnp.exp(s0 - jnp.max(s0, axis=-1, keepdims=True))
    e1 = jnp.exp(s1 - jnp.max(s1, axis=-1, keepdims=True))
    l0 = jnp.sum(e0, axis=-1, keepdims=True)
    l1 = jnp.sum(e1, axis=-1, keepdims=True)
    lp = lam_ref[...]
    lam = (jnp.exp(jnp.sum(lp[0:1] * lp[1:2], axis=-1, keepdims=True))
           - jnp.exp(jnp.sum(lp[2:3] * lp[3:4], axis=-1, keepdims=True)) + lambda_init)
    p = e0 * (1.0 / l0) - e1 * (lam / l1)
    o = _dot(p.astype(BF16), v)
    ms = jnp.mean(o * o, axis=-1, keepdims=True)
    o = o * lax.rsqrt(ms + LN_EPS) * g_ref[...] * (1.0 - lambda_init)
    o_ref[0] = o.astype(o_ref.dtype)


def diff_attention(qkv, lam_params, subln_g, lambda_init, tq):
    B, T, _ = qkv.shape
    H = DIFF_HEADS
    return pl.pallas_call(
        functools.partial(_attn_body, lambda_init=lambda_init),
        grid=(B, H, T // tq),
        in_specs=[
            pl.BlockSpec((1, tq, LANES), lambda b, h, i: (b, i, h)),
            pl.BlockSpec((1, T, LANES), lambda b, h, i: (b, 0, H + h)),
            pl.BlockSpec((1, T, LANES), lambda b, h, i: (b, 0, 2 * H + h)),
            pl.BlockSpec((4, DIFF_HEAD_DIM), lambda b, h, i: (0, 0)),
            pl.BlockSpec((1, LANES), lambda b, h, i: (0, 0)),
        ],
        out_specs=pl.BlockSpec((1, tq, LANES), lambda b, h, i: (b, i, h)),
        out_shape=jax.ShapeDtypeStruct((B, T, H * DIFF_V_DIM), BF16),
        compiler_params=_cparams(("parallel", "parallel", "arbitrary")),
        name="diff_attention",
    )(qkv, qkv, qkv, lam_params, subln_g.reshape(1, LANES))


def _per_head_sum(x, head_dim):
    lane = lax.broadcasted_iota(jnp.int32, (1, LANES), 1)
    lo = lane < head_dim
    s_lo = jnp.sum(jnp.where(lo, x, 0.0), axis=-1, keepdims=True)
    s_hi = jnp.sum(jnp.where(lo, 0.0, x), axis=-1, keepdims=True)
    return jnp.where(lo, s_lo, s_hi)


def _rwkv_pre_body(z_ref, zp_ref, zn_ref, mu_ref, w0_ref, wup_ref, a0_ref, aup_ref, gup_ref,
                   kk_ref, ka_ref, r_o, k_o, v_o, kk_o, ba_o, lw_o, g_o, *, tm):
    i = pl.program_id(1)
    last = pl.num_programs(1) - 1
    row = lax.broadcasted_iota(jnp.int32, (tm, 1), 0)
    C = RWKV_DIM

    def shifted(lo, hi):
        z = z_ref[0, :, lo:hi]
        prev = jnp.where(i == 0, 0.0, zp_ref[0, 7:8, lo:hi])
        nxt = jnp.where(i == last, 0.0, zn_ref[0, 0:1, lo:hi])
        z_dn = jnp.where(row == 0, prev, pltpu.roll(z, 1, axis=0))
        z_up = jnp.where(row == tm - 1, nxt, pltpu.roll(z, tm - 1, axis=0))
        return z + (0.5 * (z_dn + z_up) - z) * mu_ref[:, lo:hi]

    r = shifted(0, C)
    k = shifted(C, 2 * C)
    v = shifted(2 * C, 3 * C)
    wd = shifted(3 * C, 3 * C + LORA_PAD)
    ad = shifted(3 * C + LORA_PAD, 3 * C + 2 * LORA_PAD)
    gd = shifted(3 * C + 2 * LORA_PAD, 3 * C + 2 * LORA_PAD + GATE_LORA)

    r_o[0] = r
    v_o[0] = v
    twd = jnp.tanh(wd)
    for d in range(2):
        lw_o[d, 0] = -W_DECAY_SCALE * _sigmoid(w0_ref[d] + _dot(twd, wup_ref[d], HI))
    a = _sigmoid(a0_ref[...] + _dot(ad, aup_ref[...], HI))
    g_o[0] = _dot(_sigmoid(gd), gup_ref[...], HI)
    k_o[0] = k * (1.0 + (a - 1.0) * ka_ref[...])
    kk0 = k * kk_ref[...]
    for s in range(C // LANES):
        sl = slice(s * LANES, (s + 1) * LANES)
        x = kk0[:, sl]
        nrm = jnp.maximum(jnp.sqrt(_per_head_sum(x * x, RWKV_HEAD_DIM)), 1e-12)
        kk = x / nrm
        kk_o[0, :, sl] = kk
        ba_o[0, :, sl] = kk * a[:, sl]


def rwkv_pre(z, mu_p, w0, wup_p, a0, aup_p, gup, k_k, k_a, tm):
    B, T, _ = z.shape
    C = RWKV_DIM
    nb8 = tm // 8
    n8 = T // 8
    bt = jax.ShapeDtypeStruct((B, T, C), F32)
    vec = lambda t: t.reshape(1, C)
    full = lambda shp: pl.BlockSpec(shp, lambda b, i: (0,) * len(shp))
    body = functools.partial(_rwkv_pre_body, tm=tm)
    return pl.pallas_call(
        body,
        grid=(B, T // tm),
        in_specs=[
            pl.BlockSpec((1, tm, Z_A_COLS), lambda b, i: (b, i, 0)),
            pl.BlockSpec((1, 8, Z_A_COLS), lambda b, i: (b, jnp.maximum(i * nb8 - 1, 0), 0)),
            pl.BlockSpec((1, 8, Z_A_COLS), lambda b, i: (b, jnp.minimum((i + 1) * nb8, n8 - 1), 0)),
            full((1, Z_A_COLS)),
            full((2, 1, C)),
            full((2, LORA_PAD, C)),
            full((1, C)),
            full((LORA_PAD, C)),
            full((GATE_LORA, C)),
            full((1, C)),
            full((1, C)),
        ],
        out_specs=[pl.BlockSpec((1, tm, C), lambda b, i: (b, i, 0))] * 5
        + [pl.BlockSpec((2, 1, tm, C), lambda b, i: (0, b, i, 0)),
           pl.BlockSpec((1, tm, C), lambda b, i: (b, i, 0))],
        out_shape=[bt] * 5 + [jax.ShapeDtypeStruct((2, B, T, C), F32), bt],
        compiler_params=_cparams(("parallel", "parallel")),
        name="rwkv_pre",
    )(z, z, z, mu_p, w0.reshape(2, 1, C), wup_p, vec(a0), aup_p, gup, vec(k_k), vec(k_a))


def _rwkv_chunk(r, k, v, kk, ba, lw, s2, d):
    L = CHUNK
    f = jnp.float32
    ri = lax.broadcasted_iota(jnp.int32, (L, L), 0)
    ci = lax.broadcasted_iota(jnp.int32, (L, L), 1)
    sgn = 1 - 2 * d
    t_inc = jnp.where((ri - ci) * sgn >= 0, 1.0, 0.0).astype(f)
    cum = _dot(t_inc, lw, HI)
    tot = jnp.sum(lw, axis=0, keepdims=True)
    g_in = jnp.exp(cum)
    g_ex = jnp.exp(cum - lw)
    g_inv = jnp.exp(-cum)
    g_rem = jnp.exp(tot - cum)
    g_tot = jnp.exp(tot)

    lane = lax.broadcasted_iota(jnp.int32, (1, LANES), 1)
    h0 = lane < RWKV_HEAD_DIM

    def stack(x):
        return jnp.concatenate([jnp.where(h0, x, 0.0), jnp.where(h0, 0.0, x)], axis=0)

    def fold(x):
        return x[:L] + x[L:]

    a_st = stack(-kk * g_ex)
    r_st = stack(r * g_in)
    b_st = stack(ba * g_inv)
    k_st = stack(k * g_inv)
    bt_st = stack(ba * g_rem)
    kt_st = stack(k * g_rem)
    v_st = stack(v)

    r2 = lax.broadcasted_iota(jnp.int32, (2 * L, 2 * L), 0)
    c2 = lax.broadcasted_iota(jnp.int32, (2 * L, 2 * L), 1)
    order = jnp.where((r2 // L) == (c2 // L), (r2 % L - c2 % L) * sgn, -1)
    strict = order > 0
    incl = strict | (r2 == c2)

    m_ab = jnp.where(strict, _dot_nt(a_st, b_st, HI), 0.0)
    m_ak = jnp.where(strict, _dot_nt(a_st, k_st, HI), 0.0)
    m_rb = jnp.where(incl, _dot_nt(r_st, b_st, HI), 0.0)
    m_rk = jnp.where(incl, _dot_nt(r_st, k_st, HI), 0.0)

    x = jnp.concatenate([a_st, _dot(m_ak, v_st, HI)], axis=1)
    n = m_ab
    steps = int(math.log2(L))
    for it in range(steps):
        x = x + _dot(n, x, HI)
        if it + 1 < steps:
            n = _dot(n, n, HI)
    w_st = x[:, :LANES]
    ul_st = x[:, LANES:]

    q_st = r_st + _dot(m_rb, w_st, HI)
    yl_st = _dot(m_rb, ul_st, HI) + _dot(m_rk, v_st, HI)

    y_st = _dot_nt(q_st, s2, HI) + yl_st
    u_st = _dot_nt(w_st, s2, HI) + ul_st
    s2_new = s2 * g_tot + _dot_tn(u_st, bt_st, HI) + _dot_tn(v_st, kt_st, HI)
    return fold(y_st), s2_new


def _rwkv_scan_body(r_ref, k_ref, v_ref, kk_ref, ba_ref, lw_ref, y_ref, s_ref, *, n_chunks):
    d = pl.program_id(2)

    @pl.when(pl.program_id(3) == 0)
    def _():
        s_ref[...] = jnp.zeros_like(s_ref)

    def step(c, carry):
        cc = jnp.where(d == 0, c, n_chunks - 1 - c)
        rows = pl.ds(pl.multiple_of(cc * CHUNK, CHUNK), CHUNK)
        y, s_new = _rwkv_chunk(r_ref[0, rows, :], k_ref[0, rows, :], v_ref[0, rows, :],
                               kk_ref[0, rows, :], ba_ref[0, rows, :], lw_ref[0, 0, rows, :],
                               s_ref[...], d)
        y_ref[0, 0, rows, :] = y
        s_ref[...] = s_new
        return carry

    lax.fori_loop(0, n_chunks, step, 0)


def rwkv_scan(r, k, v, kk, ba, lw, tl):
    B, T, C = r.shape
    n_pairs = C // LANES
    nblk = T // tl
    tmap = lambda d, i: jnp.where(d == 0, i, nblk - 1 - i)
    spec = pl.BlockSpec((1, tl, LANES), lambda b, p, d, i: (b, tmap(d, i), p))
    spec_d = pl.BlockSpec((1, 1, tl, LANES), lambda b, p, d, i: (d, b, tmap(d, i), p))
    return pl.pallas_call(
        functools.partial(_rwkv_scan_body, n_chunks=tl // CHUNK),
        grid=(B, n_pairs, 2, nblk),
        in_specs=[spec] * 5 + [spec_d],
        out_specs=spec_d,
        out_shape=jax.ShapeDtypeStruct((2, B, T, C), F32),
        scratch_shapes=[pltpu.VMEM((LANES, LANES), F32)],
        compiler_params=_cparams(("parallel", "parallel", "parallel", "arbitrary")),
        name="rwkv_scan",
    )(r, k, v, kk, ba, lw)


def _rwkv_post_body(y_ref, r_ref, k_ref, v_ref, g_ref, rk_ref, gg_ref, gb_ref, o_ref):
    C = RWKV_DIM
    for s in range(C // LANES):
        sl = slice(s * LANES, (s + 1) * LANES)
        y = y_ref[0, 0, :, sl] + y_ref[1, 0, :, sl]
        inv_n = 1.0 / RWKV_HEAD_DIM
        mu = _per_head_sum(y, RWKV_HEAD_DIM) * inv_n
        dlt = y - mu
        var = _per_head_sum(dlt * dlt, RWKV_HEAD_DIM) * inv_n
        yn = dlt * lax.rsqrt(var + RWKV_GN_EPS) * gg_ref[:, sl] + gb_ref[:, sl]
        bonus = _per_head_sum(r_ref[0, :, sl] * k_ref[0, :, sl] * rk_ref[:, sl], RWKV_HEAD_DIM)
        out = (yn + bonus * v_ref[0, :, sl]) * g_ref[0, :, sl]
        o_ref[0, :, sl] = out.astype(o_ref.dtype)


def rwkv_post(y2, r, k, v, g, r_k, gn_g, gn_b, tm):
    B, T, C = r.shape
    spec = pl.BlockSpec((1, tm, C), lambda b, i: (b, i, 0))
    vspec = pl.BlockSpec((1, C), lambda b, i: (0, 0))
    return pl.pallas_call(
        _rwkv_post_body,
        grid=(B, T // tm),
        in_specs=[pl.BlockSpec((2, 1, tm, C), lambda b, i: (0, b, i, 0)), spec, spec, spec, spec,
                  vspec, vspec, vspec],
        out_specs=spec,
        out_shape=jax.ShapeDtypeStruct((B, T, C), BF16),
        compiler_params=_cparams(("parallel", "parallel")),
        name="rwkv_post",
    )(y2, r, k, v, g, r_k.reshape(1, C), gn_g.reshape(1, C), gn_b.reshape(1, C))


def _log_sigmoid(x):
    return jnp.minimum(x, 0.0) - jnp.log(1.0 + jnp.exp(-jnp.abs(x)))


def _gla_body(q_ref, k_ref, v_ref, gg_ref, gup_ref, gb_ref, o_ref, st_ref, *, n_chunks):
    d = pl.program_id(2)
    L = CHUNK

    @pl.when(pl.program_id(3) == 0)
    def _():
        st_ref[...] = jnp.zeros_like(st_ref)

    ri = lax.broadcasted_iota(jnp.int32, (L, L), 0)
    ci = lax.broadcasted_iota(jnp.int32, (L, L), 1)
    incl = (ri - ci) * (1 - 2 * d) >= 0
    t_inc = jnp.where(incl, 1.0, 0.0).astype(F32)
    gup = gup_ref[0]
    gb = gb_ref[0]

    def step(c, carry):
        cc = jnp.where(d == 0, c, n_chunks - 1 - c)
        rows = pl.ds(pl.multiple_of(cc * L, L), L)
        lg = _log_sigmoid(_dot(gg_ref[0, rows, :], gup, HI) + gb) * (1.0 / GLA_GATE_NORMALIZER)
        cum = _dot(t_inc, lg, HI)
        tot = jnp.sum(lg, axis=0, keepdims=True)
        q = q_ref[0, rows, :] * (GLA_DK ** -0.5)
        k = k_ref[0, rows, :]
        v = v_ref[0, rows, :].astype(BF16)
        qb = (q * jnp.exp(cum)).astype(BF16)
        kb = (k * jnp.exp(-cum)).astype(BF16)
        kt = (k * jnp.exp(tot - cum)).astype(BF16)
        st = st_ref[...]
        att = jnp.where(incl, _dot_nt(qb, kb), 0.0)
        o = _dot(att.astype(BF16), v) + _dot_nt(qb, st.astype(BF16))
        o_ref[0, 0, rows, :] = o
        st_ref[...] = st * jnp.exp(tot) + _dot_tn(v, kt)
        return carry

    lax.fori_loop(0, n_chunks, step, 0)


def gla_scan(z, gate_up_p, gate_b, tl):
    B, T, _ = z.shape
    H = GLA_HEADS
    nblk = T // tl
    tmap = lambda d, i: jnp.where(d == 0, i, nblk - 1 - i)
    qc = Z_SEGS["q"][2] // GLA_DK
    kc = Z_SEGS["k"][2] // GLA_DK
    vc = Z_SEGS["v"][2] // GLA_DV
    gc = Z_SEGS["gg"][2] // LANES
    return pl.pallas_call(
        functools.partial(_gla_body, n_chunks=tl // CHUNK),
        grid=(B, H, 2, nblk),
        in_specs=[
            pl.BlockSpec((1, tl, GLA_DK), lambda b, h, d, i: (b, tmap(d, i), qc + h)),
            pl.BlockSpec((1, tl, GLA_DK), lambda b, h, d, i: (b, tmap(d, i), kc + h)),
            pl.BlockSpec((1, tl, GLA_DV), lambda b, h, d, i: (b, tmap(d, i), vc + h)),
            pl.BlockSpec((1, tl, LANES), lambda b, h, d, i: (b, tmap(d, i), gc)),
            pl.BlockSpec((1, LANES, GLA_DK), lambda b, h, d, i: (d, 0, h)),
            pl.BlockSpec((1, 1, GLA_DK), lambda b, h, d, i: (d, 0, h)),
        ],
        out_specs=pl.BlockSpec((1, 1, tl, GLA_DV), lambda b, h, d, i: (d, b, tmap(d, i), h)),
        out_shape=jax.ShapeDtypeStruct((2, B, T, GLA_VDIM), F32),
        scratch_shapes=[pltpu.VMEM((GLA_DV, GLA_DK), F32)],
        compiler_params=_cparams(("parallel", "parallel", "parallel", "arbitrary")),
        name="gla_scan",
    )(z, z, z, z, gate_up_p, gate_b.reshape(2, 1, GLA_KDIM))


def _gla_post_body(o_ref, og_ref, g_ref, y_ref):
    o = o_ref[0, 0] + o_ref[1, 0]
    ms = jnp.mean(o * o, axis=-1, keepdims=True)
    on = o * lax.rsqrt(ms + LN_EPS) * g_ref[...]
    og = og_ref[0]
    y_ref[0] = (on * (og * _sigmoid(og))).astype(y_ref.dtype)


def gla_post(o2, z, norm_g, tm):
    _, B, T, _ = o2.shape
    H = GLA_HEADS
    oc = Z_SEGS["og"][2] // GLA_DV
    return pl.pallas_call(
        _gla_post_body,
        grid=(B, T // tm, H),
        in_specs=[
            pl.BlockSpec((2, 1, tm, GLA_DV), lambda b, i, h: (0, b, i, h)),
            pl.BlockSpec((1, tm, GLA_DV), lambda b, i, h: (b, i, oc + h)),
            pl.BlockSpec((1, GLA_DV), lambda b, i, h: (0, 0)),
        ],
        out_specs=pl.BlockSpec((1, tm, GLA_DV), lambda b, i, h: (b, i, h)),
        out_shape=jax.ShapeDtypeStruct((B, T, GLA_VDIM), BF16),
        compiler_params=_cparams(("parallel", "parallel", "parallel")),
        name="gla_post",
    )(o2, z, norm_g.reshape(1, GLA_DV))


def _pad_ab_columns(w):
    out = jnp.zeros((w.shape[0], Z_COLS), w.dtype)
    for o_start, width, n_start in Z_SEGS.values():
        out = lax.dynamic_update_slice(out, w[:, o_start:o_start + width], (0, n_start))
    return out


def _pad_rows(w, rows):
    pad = [(0, 0)] * w.ndim
    pad[-2] = (0, rows - w.shape[-2])
    return jnp.pad(w, pad)


def _qk_perm():
    half = DIFF_HEAD_DIM // 2
    perm = np.zeros(D_MODEL, np.int32)
    for h in range(DIFF_HEADS):
        for c in range(2):
            for j in range(DIFF_HEAD_DIM):
                p = (j // half) * DIFF_HEAD_DIM + c * half + j % half
                perm[h * DIFF_V_DIM + p] = h * DIFF_V_DIM + c * DIFF_HEAD_DIM + j
    return perm


def _rope_tables(T):
    half = DIFF_HEAD_DIM // 2
    inv = ROPE_THETA ** (-jnp.arange(0, DIFF_HEAD_DIM, 2, dtype=F32) / DIFF_HEAD_DIM)
    ang = jnp.arange(T, dtype=F32)[:, None] * inv[None, :]
    cos, sin = jnp.cos(ang), jnp.sin(ang)
    cos_t = jnp.tile(cos, (1, 4))
    sin_t = jnp.concatenate([-sin, -sin, sin, sin], axis=1)
    return cos_t, sin_t


def _lambda_init(layer):
    return 0.8 - 0.6 * math.exp(-0.3 * layer)


def _mixer_ab(x, shift, scale, ab_w_in, rwkv_mu, rwkv_w0, rwkv_w_up, rwkv_a0, rwkv_a_up,
              rwkv_g_up, rwkv_k_k, rwkv_k_a, rwkv_r_k, rwkv_gn_g, rwkv_gn_b,
              gla_gate_up, gla_gate_b, gla_norm_g, blocks):
    w_p = _pad_ab_columns(ab_w_in).astype(BF16)
    z = modulated_projection(x, shift, scale, w_p, F32, blocks["proj_tm"], 1024)

    mu_p = jnp.zeros((1, Z_A_COLS), F32)
    for name in ("rkv", "wd", "ad", "gd"):
        o_start, width, n_start = Z_SEGS[name]
        mu_p = lax.dynamic_update_slice(mu_p, rwkv_mu[None, o_start:o_start + width], (0, n_start))
    r, k, v, kk, ba, lw, g = rwkv_pre(
        z, mu_p, rwkv_w0, _pad_rows(rwkv_w_up, LORA_PAD), rwkv_a0, _pad_rows(rwkv_a_up, LORA_PAD),
        rwkv_g_up, rwkv_k_k, rwkv_k_a, blocks["pre_tm"])
    y2 = rwkv_scan(r, k, v, kk, ba, lw, blocks["scan_tl"])
    y_a = rwkv_post(y2, r, k, v, g, rwkv_r_k, rwkv_gn_g, rwkv_gn_b, blocks["pre_tm"])

    o2 = gla_scan(z, _pad_rows(gla_gate_up, LANES), gla_gate_b, blocks["scan_tl"])
    y_b = gla_post(o2, z, gla_norm_g, blocks["pre_tm"])
    return y_a, y_b


def _mixer_c(x, shift, scale, diff_w_in, diff_lambda, diff_subln_g, lambda_init, blocks):
    T = x.shape[1]
    perm = _qk_perm()
    w = jnp.concatenate([diff_w_in[:, :D_MODEL][:, perm],
                         diff_w_in[:, D_MODEL:2 * D_MODEL][:, perm],
                         diff_w_in[:, 2 * D_MODEL:]], axis=1).astype(BF16)
    cos_t, sin_t = _rope_tables(T)
    qkv = qkv_projection(x, shift, scale, w, cos_t, sin_t, blocks["proj_tm"], 1024)
    return diff_attention(qkv, diff_lambda, diff_subln_g, lambda_init, blocks["attn_tq"])


def _blocks(T):
    return {
        "proj_tm": min(1024, T),
        "pre_tm": min(256, T),
        "scan_tl": min(256, T),
        "attn_tq": min(256, T),
        "ln_tm": min(512, T),
        "ffn_tm": min(512, T),
    }


def kernel(x, c, ada_w, ada_b, ln_g, ln_b, ffn_w_in, ffn_w_out, ab_w_in, ab_w_out, rwkv_mu, rwkv_w0, rwkv_w_up, rwkv_a0, rwkv_a_up, rwkv_g_up, rwkv_k_k, rwkv_k_a, rwkv_r_k, rwkv_gn_g, rwkv_gn_b, gla_gate_up, gla_gate_b, gla_norm_g, diff_w_in, diff_w_out, diff_lambda, diff_subln_g):
    blocks = _blocks(x.shape[1])
    shift, scale, gate = ada_modulation(c, ada_w, ada_b)
    for i in range(DEPTH):
        j = i // 2
        m = 2 * i
        if i % 2 == 0:
            y_parts = _mixer_ab(
                x, shift[m], scale[m], ab_w_in[j], rwkv_mu[j], rwkv_w0[j], rwkv_w_up[j],
                rwkv_a0[j], rwkv_a_up[j], rwkv_g_up[j], rwkv_k_k[j], rwkv_k_a[j], rwkv_r_k[j],
                rwkv_gn_g[j], rwkv_gn_b[j], gla_gate_up[j], gla_gate_b[j], gla_norm_g[j], blocks)
            w_out = ab_w_out[j]
        else:
            y_parts = (_mixer_c(x, shift[m], scale[m], diff_w_in[j], diff_lambda[j],
                                diff_subln_g[j], _lambda_init(i), blocks),)
            w_out = diff_w_out[j]
        x = projection_layernorm(y_parts, w_out.astype(BF16), x, gate[m], ln_g[i, 0], ln_b[i, 0],
                                 blocks["ln_tm"])
        x = ffn_sublayer(x, shift[m + 1], scale[m + 1], gate[m + 1], ffn_w_in[i].astype(BF16),
                         ffn_w_out[i].astype(BF16), ln_g[i, 1], ln_b[i, 1], blocks["ffn_tm"], 512)
    return x
```

```python
import functools
import math

import numpy as np
import jax
import jax.numpy as jnp
from jax import lax
from jax.experimental import pallas as pl
from jax.experimental.pallas import tpu as pltpu

F32 = jnp.float32
BF16 = jnp.bfloat16
HI = lax.Precision.HIGHEST

D_MODEL = 2048
DEPTH = 2
ALPHA = (2.0 * DEPTH) ** 0.25
LN_EPS = 1e-5

RWKV_HEAD_DIM = 64
RWKV_DIM = D_MODEL // 2
RWKV_HEADS = RWKV_DIM // RWKV_HEAD_DIM
DECAY_LORA = 96
ICL_LORA = 96
GATE_LORA = 256
W_DECAY_SCALE = 0.606531
RWKV_GN_EPS = 64e-5

GLA_HEADS = 4
GLA_VDIM = D_MODEL // 2
GLA_KDIM = GLA_VDIM // 2
GLA_DK = GLA_KDIM // GLA_HEADS
GLA_DV = GLA_VDIM // GLA_HEADS
GLA_GATE_LORA = 16
GLA_GATE_NORMALIZER = 16.0

DIFF_HEAD_DIM = 64
DIFF_V_DIM = 2 * DIFF_HEAD_DIM
DIFF_HEADS = D_MODEL // DIFF_V_DIM
ROPE_THETA = 10000.0

D_FF = -(-(8 * D_MODEL) // (3 * 256)) * 256

LANES = 128
VMEM_LIMIT = 56 * 1024 * 1024

CHUNK = 64
LORA_PAD = 128
Z_SEGS = {
    "rkv": (0, 3 * RWKV_DIM, 0),
    "wd": (3 * RWKV_DIM, DECAY_LORA, 3072),
    "ad": (3 * RWKV_DIM + DECAY_LORA, ICL_LORA, 3200),
    "gd": (3 * RWKV_DIM + DECAY_LORA + ICL_LORA, GATE_LORA, 3328),
    "q": (3520, GLA_KDIM, 3584),
    "k": (3520 + GLA_KDIM, GLA_KDIM, 4096),
    "v": (3520 + 2 * GLA_KDIM, GLA_VDIM, 4608),
    "gg": (3520 + 2 * GLA_KDIM + GLA_VDIM, GLA_GATE_LORA, 6656),
    "og": (3520 + 2 * GLA_KDIM + GLA_VDIM + GLA_GATE_LORA, GLA_VDIM, 5632),
}
Z_A_COLS = 3584
Z_COLS = 7168


def _cparams(sem):
    return pltpu.CompilerParams(dimension_semantics=sem, vmem_limit_bytes=VMEM_LIMIT)


def _dot(a, b, prec=None):
    return jnp.dot(a, b, preferred_element_type=F32, precision=prec)


def _dot_nt(a, b, prec=None):
    return lax.dot_general(a, b, (((1,), (1,)), ((), ())), preferred_element_type=F32,
                           precision=prec)


def _dot_tn(a, b, prec=None):
    return lax.dot_general(a, b, (((0,), (0,)), ((), ())), preferred_element_type=F32,
                           precision=prec)


def _sigmoid(x):
    return 1.0 / (1.0 + jnp.exp(-x))


def _layer_norm_rows(u, g, b, eps):
    mu = jnp.mean(u, axis=-1, keepdims=True)
    d = u - mu
    var = jnp.mean(d * d, axis=-1, keepdims=True)
    return d * lax.rsqrt(var + eps) * g + b


def _ada_body(c_ref, w_ref, b_ref, o_ref):
    c = c_ref[...]
    sc = (c * _sigmoid(c)).astype(BF16)
    o_ref[0] = _dot(sc, w_ref[0].astype(BF16)) + b_ref[0]


def ada_modulation(c, ada_w, ada_b):
    B, D = c.shape
    n = ada_w.shape[0] * ada_w.shape[1]
    w = ada_w.reshape(n, D, 3 * D)
    b = ada_b.reshape(n, 1, 3 * D)
    rows = 8
    c_pad = jnp.pad(c, ((0, rows - B), (0, 0)))
    tn = 1024
    out = pl.pallas_call(
        _ada_body,
        grid=(n, 3 * D // tn),
        in_specs=[
            pl.BlockSpec((rows, D), lambda i, j: (0, 0)),
            pl.BlockSpec((1, D, tn), lambda i, j: (i, 0, j)),
            pl.BlockSpec((1, 1, tn), lambda i, j: (i, 0, j)),
        ],
        out_specs=pl.BlockSpec((1, rows, tn), lambda i, j: (i, 0, j)),
        out_shape=jax.ShapeDtypeStruct((n, rows, 3 * D), F32),
        compiler_params=_cparams(("parallel", "parallel")),
        name="ada_modulation",
    )(c_pad, w, b)
    mods = out[:, :B, :]
    shift, scale, gate = mods[..., :D], mods[..., D:2 * D], mods[..., 2 * D:]
    r3 = lambda t: t.reshape(n, B, 1, D)
    return r3(shift), r3(scale), r3(gate)


def _inproj_body(x_ref, sh_ref, sc_ref, w_ref, o_ref, h_ref):
    @pl.when(pl.program_id(2) == 0)
    def _():
        h_ref[...] = (x_ref[0] * (1.0 + sc_ref[0]) + sh_ref[0]).astype(BF16)

    o_ref[0] = _dot(h_ref[...], w_ref[...]).astype(o_ref.dtype)


def modulated_projection(x, shift, scale, w, out_dtype, tm, tn):
    B, T, D = x.shape
    N = w.shape[1]
    return pl.pallas_call(
        _inproj_body,
        grid=(B, T // tm, N // tn),
        in_specs=[
            pl.BlockSpec((1, tm, D), lambda b, i, j: (b, i, 0)),
            pl.BlockSpec((1, 1, D), lambda b, i, j: (b, 0, 0)),
            pl.BlockSpec((1, 1, D), lambda b, i, j: (b, 0, 0)),
            pl.BlockSpec((D, tn), lambda b, i, j: (0, j)),
        ],
        out_specs=pl.BlockSpec((1, tm, tn), lambda b, i, j: (b, i, j)),
        out_shape=jax.ShapeDtypeStruct((B, T, N), out_dtype),
        scratch_shapes=[pltpu.VMEM((tm, D), BF16)],
        compiler_params=_cparams(("parallel", "parallel", "arbitrary")),
        name="modulated_projection",
    )(x, shift, scale, w)


def _qkv_body(x_ref, sh_ref, sc_ref, w_ref, cos_ref, sin_ref, o_ref, h_ref, *, n_q, n_qk,
              q_scale):
    j = pl.program_id(2)

    @pl.when(j == 0)
    def _():
        h_ref[...] = (x_ref[0] * (1.0 + sc_ref[0]) + sh_ref[0]).astype(BF16)

    acc = _dot(h_ref[...], w_ref[...])
    tn = acc.shape[1]

    @pl.when(j < n_qk)
    def _():
        cos = cos_ref[...]
        sin = sin_ref[...]
        mult = jnp.where(j < n_q, q_scale, 1.0).astype(F32)
        for s in range(tn // LANES):
            xs = acc[:, s * LANES:(s + 1) * LANES]
            rot = pltpu.roll(xs, LANES // 2, axis=1)
            o_ref[0, :, s * LANES:(s + 1) * LANES] = ((xs * cos + rot * sin) * mult).astype(o_ref.dtype)

    @pl.when(j >= n_qk)
    def _():
        o_ref[0] = acc.astype(o_ref.dtype)


def qkv_projection(x, shift, scale, w, cos_t, sin_t, tm, tn):
    B, T, D = x.shape
    N = w.shape[1]
    n_q = D_MODEL // tn
    body = functools.partial(_qkv_body, n_q=n_q, n_qk=2 * n_q,
                             q_scale=DIFF_HEAD_DIM ** -0.5 * math.log2(math.e))
    return pl.pallas_call(
        body,
        grid=(B, T // tm, N // tn),
        in_specs=[
            pl.BlockSpec((1, tm, D), lambda b, i, j: (b, i, 0)),
            pl.BlockSpec((1, 1, D), lambda b, i, j: (b, 0, 0)),
            pl.BlockSpec((1, 1, D), lambda b, i, j: (b, 0, 0)),
            pl.BlockSpec((D, tn), lambda b, i, j: (0, j)),
            pl.BlockSpec((tm, LANES), lambda b, i, j: (i, 0)),
            pl.BlockSpec((tm, LANES), lambda b, i, j: (i, 0)),
        ],
        out_specs=pl.BlockSpec((1, tm, tn), lambda b, i, j: (b, i, j)),
        out_shape=jax.ShapeDtypeStruct((B, T, N), BF16),
        scratch_shapes=[pltpu.VMEM((tm, D), BF16)],
        compiler_params=_cparams(("parallel", "parallel", "arbitrary")),
        name="qkv_projection",
    )(x, shift, scale, w, cos_t, sin_t)


def _proj_ln_body(*refs, n_parts):
    y_refs = refs[:n_parts]
    w_ref, x_ref, gate_ref, g_ref, b_ref, o_ref = refs[n_parts:]
    acc = None
    off = 0
    for yr in y_refs:
        k = yr.shape[-1]
        part = _dot(yr[0], w_ref[off:off + k, :])
        acc = part if acc is None else acc + part
        off += k
    u = ALPHA * x_ref[0] + (1.0 + gate_ref[0]) * acc
    o_ref[0] = _layer_norm_rows(u, g_ref[...], b_ref[...], LN_EPS)


def projection_layernorm(y_parts, w, x, gate, ln_g, ln_b, tm):
    B, T, D = x.shape
    K = w.shape[0]
    in_specs = [pl.BlockSpec((1, tm, yp.shape[-1]), lambda b, i: (b, i, 0)) for yp in y_parts]
    in_specs += [
        pl.BlockSpec((K, D), lambda b, i: (0, 0)),
        pl.BlockSpec((1, tm, D), lambda b, i: (b, i, 0)),
        pl.BlockSpec((1, 1, D), lambda b, i: (b, 0, 0)),
        pl.BlockSpec((1, D), lambda b, i: (0, 0)),
        pl.BlockSpec((1, D), lambda b, i: (0, 0)),
    ]
    return pl.pallas_call(
        functools.partial(_proj_ln_body, n_parts=len(y_parts)),
        grid=(B, T // tm),
        in_specs=in_specs,
        out_specs=pl.BlockSpec((1, tm, D), lambda b, i: (b, i, 0)),
        out_shape=jax.ShapeDtypeStruct((B, T, D), F32),
        compiler_params=_cparams(("parallel", "parallel")),
        name="projection_layernorm",
    )(*y_parts, w, x, gate, ln_g.reshape(1, D), ln_b.reshape(1, D))


def _ffn_body(x_ref, sh_ref, sc_ref, gate_ref, wg_ref, wu_ref, wo_ref, g_ref, b_ref, o_ref,
              h_ref, acc_ref):
    j = pl.program_id(2)

    @pl.when(j == 0)
    def _():
        h_ref[...] = (x_ref[0] * (1.0 + sc_ref[0]) + sh_ref[0]).astype(BF16)
        acc_ref[...] = jnp.zeros_like(acc_ref)

    h = h_ref[...]
    gt = _dot(h, wg_ref[...])
    up = _dot(h, wu_ref[...])
    act = (gt * _sigmoid(gt) * up).astype(BF16)
    acc_ref[...] += _dot(act, wo_ref[...])

    @pl.when(j == pl.num_programs(2) - 1)
    def _():
        u = ALPHA * x_ref[0] + (1.0 + gate_ref[0]) * acc_ref[...]
        o_ref[0] = _layer_norm_rows(u, g_ref[...], b_ref[...], LN_EPS)


def ffn_sublayer(x, shift, scale, gate, w_in, w_out, ln_g, ln_b, tm, tf):
    B, T, D = x.shape
    F = w_out.shape[0]
    nf = F // tf
    return pl.pallas_call(
        _ffn_body,
        grid=(B, T // tm, nf),
        in_specs=[
            pl.BlockSpec((1, tm, D), lambda b, i, j: (b, i, 0)),
            pl.BlockSpec((1, 1, D), lambda b, i, j: (b, 0, 0)),
            pl.BlockSpec((1, 1, D), lambda b, i, j: (b, 0, 0)),
            pl.BlockSpec((1, 1, D), lambda b, i, j: (b, 0, 0)),
            pl.BlockSpec((D, tf), lambda b, i, j: (0, j)),
            pl.BlockSpec((D, tf), lambda b, i, j: (0, j + nf)),
            pl.BlockSpec((tf, D), lambda b, i, j: (j, 0)),
            pl.BlockSpec((1, D), lambda b, i, j: (0, 0)),
            pl.BlockSpec((1, D), lambda b, i, j: (0, 0)),
        ],
        out_specs=pl.BlockSpec((1, tm, D), lambda b, i, j: (b, i, 0)),
        out_shape=jax.ShapeDtypeStruct((B, T, D), F32),
        scratch_shapes=[pltpu.VMEM((tm, D), BF16), pltpu.VMEM((tm, D), F32)],
        compiler_params=_cparams(("parallel", "parallel", "arbitrary")),
        name="ffn_sublayer",
    )(x, shift, scale, gate, w_in, w_in, w_out, ln_g.reshape(1, D), ln_b.reshape(1, D))


ONES_ROWS = 16


def _attn_body(q0_ref, qb_ref, qn_ref, k_ref, v_ref, lam_ref, g_ref, o_ref, vt_ref, sa_ref,
               *, lambda_init, tq):
    dv = DIFF_V_DIM
    k = k_ref[0]
    lane = lax.broadcasted_iota(jnp.int32, (1, LANES), 1)
    comp0 = (lane // (DIFF_HEAD_DIM // 2)) % 2 == 0

    def scores(q):
        zero = jnp.zeros_like(q)
        return [_dot_nt(k, qc) for qc in (jnp.where(comp0, q, zero), jnp.where(comp0, zero, q))]

    @pl.when(pl.program_id(2) == 0)
    def _():
        vt_ref[0:dv, :] = v_ref[0].astype(F32).T.astype(BF16)
        vt_ref[dv:, :] = jnp.ones((ONES_ROWS, vt_ref.shape[1]), BF16)
        s = scores(q0_ref[0])
        sa_ref[0] = s[0]
        sa_ref[1] = s[1]

    lp = lam_ref[...]
    lam = (jnp.exp(jnp.sum(lp[0:1] * lp[1:2], axis=-1, keepdims=True))
           - jnp.exp(jnp.sum(lp[2:3] * lp[3:4], axis=-1, keepdims=True)) + lambda_init)

    def finish(ss):
        vt = vt_ref[...]
        es = [jnp.exp2(s - jnp.max(s, axis=0, keepdims=True)).astype(BF16) for s in ss]
        ols = [_dot(vt, e) for e in es]
        outs = [ol[0:dv] / ol[dv:dv + 1] for ol in ols]
        o = (outs[0] - lam * outs[1]).T
        ms = jnp.mean(o * o, axis=-1, keepdims=True)
        o = o * lax.rsqrt(ms + LN_EPS) * g_ref[...] * (1.0 - lambda_init)
        return o.astype(o_ref.dtype)

    s_b = scores(qb_ref[0])
    o_ref[0, 0:tq, :] = finish([sa_ref[0], sa_ref[1]])
    s_n = scores(qn_ref[0])
    sa_ref[0] = s_n[0]
    sa_ref[1] = s_n[1]
    o_ref[0, tq:2 * tq, :] = finish(s_b)


def diff_attention(qkv, lam_params, subln_g, lambda_init, tq):
    B, T, _ = qkv.shape
    H = DIFF_HEADS
    nq = T // tq
    return pl.pallas_call(
        functools.partial(_attn_body, lambda_init=lambda_init, tq=tq),
        grid=(B, H, nq // 2),
        in_specs=[
            pl.BlockSpec((1, tq, LANES), lambda b, h, i: (b, 0, h)),
            pl.BlockSpec((1, tq, LANES), lambda b, h, i: (b, 2 * i + 1, h)),
            pl.BlockSpec((1, tq, LANES), lambda b, h, i: (b, jnp.minimum(2 * i + 2, nq - 1), h)),
            pl.BlockSpec((1, T, LANES), lambda b, h, i: (b, 0, H + h)),
            pl.BlockSpec((1, T, LANES), lambda b, h, i: (b, 0, 2 * H + h)),
            pl.BlockSpec((4, DIFF_HEAD_DIM), lambda b, h, i: (0, 0)),
            pl.BlockSpec((1, LANES), lambda b, h, i: (0, 0)),
        ],
        out_specs=pl.BlockSpec((1, 2 * tq, LANES), lambda b, h, i: (b, i, h)),
        out_shape=jax.ShapeDtypeStruct((B, T, H * DIFF_V_DIM), BF16),
        scratch_shapes=[pltpu.VMEM((DIFF_V_DIM + ONES_ROWS, T), BF16),
                        pltpu.VMEM((2, T, tq), F32)],
        compiler_params=_cparams(("parallel", "parallel", "arbitrary")),
        name="diff_attention",
    )(qkv, qkv, qkv, qkv, qkv, lam_params, subln_g.reshape(1, LANES))


def _per_head_sum(x, head_dim):
    lane = lax.broadcasted_iota(jnp.int32, (1, LANES), 1)
    lo = lane < head_dim
    s_lo = jnp.sum(jnp.where(lo, x, 0.0), axis=-1, keepdims=True)
    s_hi = jnp.sum(jnp.where(lo, 0.0, x), axis=-1, keepdims=True)
    return jnp.where(lo, s_lo, s_hi)


def _rwkv_pre_body(z_ref, zp_ref, zn_ref, mu_ref, w0_ref, wup_ref, a0_ref, aup_ref, gup_ref,
                   kk_ref, ka_ref, r_o, k_o, v_o, kk_o, ba_o, lw_o, g_o, *, tm):
    i = pl.program_id(1)
    last = pl.num_programs(1) - 1
    row = lax.broadcasted_iota(jnp.int32, (tm, 1), 0)
    C = RWKV_DIM

    def shifted(lo, hi):
        z = z_ref[0, :, lo:hi]
        prev = jnp.where(i == 0, 0.0, zp_ref[0, 7:8, lo:hi])
        nxt = jnp.where(i == last, 0.0, zn_ref[0, 0:1, lo:hi])
        z_dn = jnp.where(row == 0, prev, pltpu.roll(z, 1, axis=0))
        z_up = jnp.where(row == tm - 1, nxt, pltpu.roll(z, tm - 1, axis=0))
        return z + (0.5 * (z_dn + z_up) - z) * mu_ref[:, lo:hi]

    r = shifted(0, C)
    k = shifted(C, 2 * C)
    v = shifted(2 * C, 3 * C)
    wd = shifted(3 * C, 3 * C + LORA_PAD)
    ad = shifted(3 * C + LORA_PAD, 3 * C + 2 * LORA_PAD)
    gd = shifted(3 * C + 2 * LORA_PAD, 3 * C + 2 * LORA_PAD + GATE_LORA)

    r_o[0] = r
    v_o[0] = v
    twd = jnp.tanh(wd).astype(BF16)
    for d in range(2):
        lw_o[d, 0] = -W_DECAY_SCALE * _sigmoid(w0_ref[d] + _dot(twd, wup_ref[d].astype(BF16)))
    a = _sigmoid(a0_ref[...] + _dot(ad.astype(BF16), aup_ref[...].astype(BF16)))
    g_o[0] = _dot(_sigmoid(gd).astype(BF16), gup_ref[...].astype(BF16))
    k_o[0] = k * (1.0 + (a - 1.0) * ka_ref[...])
    kk0 = k * kk_ref[...]
    for s in range(C // LANES):
        sl = slice(s * LANES, (s + 1) * LANES)
        x = kk0[:, sl]
        nrm = jnp.maximum(jnp.sqrt(_per_head_sum(x * x, RWKV_HEAD_DIM)), 1e-12)
        kk = x / nrm
        kk_o[0, :, sl] = kk
        ba_o[0, :, sl] = kk * a[:, sl]


def rwkv_pre(z, mu_p, w0, wup_p, a0, aup_p, gup, k_k, k_a, tm):
    B, T, _ = z.shape
    C = RWKV_DIM
    nb8 = tm // 8
    n8 = T // 8
    bt = jax.ShapeDtypeStruct((B, T, C), F32)
    vec = lambda t: t.reshape(1, C)
    full = lambda shp: pl.BlockSpec(shp, lambda b, i: (0,) * len(shp))
    body = functools.partial(_rwkv_pre_body, tm=tm)
    return pl.pallas_call(
        body,
        grid=(B, T // tm),
        in_specs=[
            pl.BlockSpec((1, tm, Z_A_COLS), lambda b, i: (b, i, 0)),
            pl.BlockSpec((1, 8, Z_A_COLS), lambda b, i: (b, jnp.maximum(i * nb8 - 1, 0), 0)),
            pl.BlockSpec((1, 8, Z_A_COLS), lambda b, i: (b, jnp.minimum((i + 1) * nb8, n8 - 1), 0)),
            full((1, Z_A_COLS)),
            full((2, 1, C)),
            full((2, LORA_PAD, C)),
            full((1, C)),
            full((LORA_PAD, C)),
            full((GATE_LORA, C)),
            full((1, C)),
            full((1, C)),
        ],
        out_specs=[pl.BlockSpec((1, tm, C), lambda b, i: (b, i, 0))] * 5
        + [pl.BlockSpec((2, 1, tm, C), lambda b, i: (0, b, i, 0)),
           pl.BlockSpec((1, tm, C), lambda b, i: (b, i, 0))],
        out_shape=[bt] * 5 + [jax.ShapeDtypeStruct((2, B, T, C), F32), bt],
        compiler_params=_cparams(("parallel", "parallel")),
        name="rwkv_pre",
    )(z, z, z, mu_p, w0.reshape(2, 1, C), wup_p, vec(a0), aup_p, gup, vec(k_k), vec(k_a))


def _bf(x):
    return x.astype(BF16)


def _segmented_cumsum(x, seg):
    pos = lax.broadcasted_iota(jnp.int32, (x.shape[0], 1), 0) % seg
    s = 1
    while s < seg:
        x = x + jnp.where(pos >= s, pltpu.roll(x, s, axis=0), 0.0)
        s *= 2
    return x


def _segment_totals(x, seg):
    parts = []
    for c in range(x.shape[0] // seg):
        t = jnp.sum(x[c * seg:(c + 1) * seg], axis=0, keepdims=True)
        parts.append(jnp.broadcast_to(t, (seg, x.shape[1])))
    return jnp.concatenate(parts, axis=0)


def _decay_factors(lw, reverse):
    tot = _segment_totals(lw, CHUNK)
    cum = _segmented_cumsum(lw, CHUNK)
    if reverse:
        cum = tot - cum + lw
    return cum, tot


def _rwkv_chunks_local(chunks):
    L = CHUNK
    lane = lax.broadcasted_iota(jnp.int32, (1, LANES), 1)
    h0 = lane < RWKV_HEAD_DIM

    def stack(x):
        return jnp.concatenate([jnp.where(h0, x, 0.0), jnp.where(h0, 0.0, x)], axis=0)

    pre = []
    for r, k, v, kk, ba, lw, cum, tot, strict, incl in chunks:
        g_ex = jnp.exp(cum - lw)
        g_inv = jnp.exp(-cum)
        g_rem = jnp.exp(tot - cum)
        a_st = stack(-kk * g_ex)
        r_st = stack(r * jnp.exp(cum))
        lhs = _bf(jnp.concatenate([a_st, r_st], axis=0))
        rhs = _bf(jnp.concatenate([stack(ba * g_inv), stack(k * g_inv)], axis=0))
        t_b = _bf(jnp.concatenate([stack(ba * g_rem), stack(k * g_rem)], axis=0))
        pre.append((a_st, r_st, lhs, rhs, t_b, _bf(stack(v))))

    ps = [_dot_nt(lhs, rhs) for _, _, lhs, rhs, _, _ in pre]
    m_ab, m_ak, m_r = [], [], []
    for p, ch in zip(ps, chunks):
        strict, incl = ch[8], ch[9]
        m_ab.append(jnp.where(strict, p[:2 * L, :2 * L], 0.0))
        m_ak.append(_bf(jnp.where(strict, p[:2 * L, 2 * L:], 0.0)))
        m_r.append(_bf(jnp.concatenate([jnp.where(incl, p[2 * L:, :2 * L], 0.0),
                                        jnp.where(incl, p[2 * L:, 2 * L:], 0.0)], axis=1)))

    akv = [_dot(m, pr[5]) for m, pr in zip(m_ak, pre)]
    xs = [jnp.concatenate([pr[0], t], axis=1) for pr, t in zip(pre, akv)]
    ns = m_ab
    steps = int(math.log2(L))
    for it in range(steps):
        n_bs = [_bf(n) for n in ns]
        xs = [x + _dot(n_b, _bf(x)) for x, n_b in zip(xs, n_bs)]
        if it + 1 < steps:
            ns = [_dot(n_b, n_b) for n_b in n_bs]

    out = []
    wu = [(_bf(x[:, :LANES]), _bf(x[:, LANES:])) for x in xs]
    qys = [_dot(m, jnp.concatenate([jnp.concatenate([w_b, ul_b], axis=1),
                                    jnp.concatenate([jnp.zeros_like(pr[5]), pr[5]], axis=1)], axis=0))
           for m, (w_b, ul_b), pr in zip(m_r, wu, pre)]
    gps = [_dot_tn(w_b, pr[4][:2 * L]) for (w_b, _), pr in zip(wu, pre)]
    hs = [_dot_tn(jnp.concatenate([ul_b, pr[5]], axis=0), pr[4])
          for (_, ul_b), pr in zip(wu, pre)]
    for qy, gp, h, pr in zip(qys, gps, hs, pre):
        out.append((_bf(pr[1] + qy[:, :LANES]), qy[:, LANES:], _bf(gp), h))
    return out


def _rwkv_scan_body(rf, kf, vf, kkf, baf, lwf, rb, kb, vb, kkb, bab, lwb, yf_ref, yb_ref, s_ref,
                    *, n_chunks):
    L = CHUNK

    @pl.when(pl.program_id(2) == 0)
    def _():
        s_ref[...] = jnp.zeros_like(s_ref)

    r2 = lax.broadcasted_iota(jnp.int32, (2 * L, 2 * L), 0)
    c2 = lax.broadcasted_iota(jnp.int32, (2 * L, 2 * L), 1)
    same_head = (r2 // L) == (c2 // L)
    dt = r2 % L - c2 % L

    chunks, g_tots = [], []
    for d, refs in ((0, (rf, kf, vf, kkf, baf, lwf)), (1, (rb, kb, vb, kkb, bab, lwb))):
        r_ref, k_ref, v_ref, kk_ref, ba_ref, lw_ref = refs
        strict = jnp.where(same_head, dt if d == 0 else -dt, -1) > 0
        incl = strict | (r2 == c2)
        lw_all = lw_ref[0, 0]
        cum_all, tot_all = _decay_factors(lw_all, reverse=(d == 1))
        g_tot_all = jnp.exp(tot_all)
        for c in range(n_chunks):
            rows = slice(c * L, (c + 1) * L)
            chunks.append((r_ref[0, rows, :], k_ref[0, rows, :], v_ref[0, rows, :],
                           kk_ref[0, rows, :], ba_ref[0, rows, :], lw_all[rows], cum_all[rows],
                           tot_all[rows], strict, incl))
            g_tots.append(g_tot_all[c * L:c * L + 1])
    local = _rwkv_chunks_local(chunks)

    s2 = [s_ref[0], s_ref[1]]
    for j in range(n_chunks):
        for d, y_ref in ((0, yf_ref), (1, yb_ref)):
            c = j if d == 0 else n_chunks - 1 - j
            q_b, yl_st, gp_b, h = local[d * n_chunks + c]
            s_b = _bf(s2[d])
            y_st = _dot_nt(q_b, s_b) + yl_st
            y_ref[0, c * L:(c + 1) * L, :] = y_st[:L] + y_st[L:]
            s2[d] = s2[d] * g_tots[d * n_chunks + c] + _dot(s_b, gp_b) + h
    s_ref[0] = s2[0]
    s_ref[1] = s2[1]


def rwkv_scan(r, k, v, kk, ba, lw, tl):
    B, T, C = r.shape
    n_pairs = C // LANES
    nblk = T // tl
    fwd = pl.BlockSpec((1, tl, LANES), lambda b, p, i: (b, i, p))
    bwd = pl.BlockSpec((1, tl, LANES), lambda b, p, i: (b, nblk - 1 - i, p))
    lw_f = pl.BlockSpec((1, 1, tl, LANES), lambda b, p, i: (0, b, i, p))
    lw_b = pl.BlockSpec((1, 1, tl, LANES), lambda b, p, i: (1, b, nblk - 1 - i, p))
    y_sds = jax.ShapeDtypeStruct((B, T, C), F32)
    return pl.pallas_call(
        functools.partial(_rwkv_scan_body, n_chunks=tl // CHUNK),
        grid=(B, n_pairs, nblk),
        in_specs=[fwd] * 5 + [lw_f] + [bwd] * 5 + [lw_b],
        out_specs=[fwd, bwd],
        out_shape=[y_sds, y_sds],
        scratch_shapes=[pltpu.VMEM((2, LANES, LANES), F32)],
        compiler_params=_cparams(("parallel", "parallel", "arbitrary")),
        name="rwkv_scan",
    )(r, k, v, kk, ba, lw, r, k, v, kk, ba, lw)


def _rwkv_post_body(yf_ref, yb_ref, r_ref, k_ref, v_ref, g_ref, rk_ref, gg_ref, gb_ref, o_ref):
    C = RWKV_DIM
    for s in range(C // LANES):
        sl = slice(s * LANES, (s + 1) * LANES)
        y = yf_ref[0, :, sl] + yb_ref[0, :, sl]
        inv_n = 1.0 / RWKV_HEAD_DIM
        mu = _per_head_sum(y, RWKV_HEAD_DIM) * inv_n
        dlt = y - mu
        var = _per_head_sum(dlt * dlt, RWKV_HEAD_DIM) * inv_n
        yn = dlt * lax.rsqrt(var + RWKV_GN_EPS) * gg_ref[:, sl] + gb_ref[:, sl]
        bonus = _per_head_sum(r_ref[0, :, sl] * k_ref[0, :, sl] * rk_ref[:, sl], RWKV_HEAD_DIM)
        out = (yn + bonus * v_ref[0, :, sl]) * g_ref[0, :, sl]
        o_ref[0, :, sl] = out.astype(o_ref.dtype)


def rwkv_post(yf, yb, r, k, v, g, r_k, gn_g, gn_b, tm):
    B, T, C = r.shape
    spec = pl.BlockSpec((1, tm, C), lambda b, i: (b, i, 0))
    vspec = pl.BlockSpec((1, C), lambda b, i: (0, 0))
    return pl.pallas_call(
        _rwkv_post_body,
        grid=(B, T // tm),
        in_specs=[spec] * 6 + [vspec] * 3,
        out_specs=spec,
        out_shape=jax.ShapeDtypeStruct((B, T, C), BF16),
        compiler_params=_cparams(("parallel", "parallel")),
        name="rwkv_post",
    )(yf, yb, r, k, v, g, r_k.reshape(1, C), gn_g.reshape(1, C), gn_b.reshape(1, C))


def _log_sigmoid(x):
    return jnp.minimum(x, 0.0) - jnp.log(1.0 + jnp.exp(-jnp.abs(x)))


def _gla_body(qf, kf, vf, ggf, qb_ref, kb_ref, vb_ref, ggb, gup_ref, gb_ref, of_ref, ob_ref, st_ref,
              *, n_chunks):
    L = CHUNK
    tl = n_chunks * L

    @pl.when(pl.program_id(2) == 0)
    def _():
        st_ref[...] = jnp.zeros_like(st_ref)

    ri = lax.broadcasted_iota(jnp.int32, (tl, tl), 0)
    ci = lax.broadcasted_iota(jnp.int32, (tl, tl), 1)
    same_chunk = (ri // L) == (ci // L)

    for d, refs, o_ref in ((0, (qf, kf, vf, ggf), of_ref), (1, (qb_ref, kb_ref, vb_ref, ggb), ob_ref)):
        q_ref, k_ref, v_ref, gg_ref = refs
        incl = jnp.where(same_chunk, (ri - ci) if d == 0 else (ci - ri), -1) >= 0
        lg = _log_sigmoid(_dot(_bf(gg_ref[0]), _bf(gup_ref[d])) + gb_ref[d]) * (1.0 / GLA_GATE_NORMALIZER)
        cum, tot = _decay_factors(lg, reverse=(d == 1))
        k = k_ref[0]
        v = _bf(v_ref[0])
        qb = _bf(q_ref[0] * (GLA_DK ** -0.5) * jnp.exp(cum))
        kb = _bf(k * jnp.exp(-cum))
        kt = _bf(k * jnp.exp(tot - cum))
        att = jnp.where(incl, _dot_nt(qb, kb), 0.0)
        o_intra = _dot(_bf(att), v)
        st = st_ref[d]
        order = range(n_chunks) if d == 0 else reversed(range(n_chunks))
        for c in order:
            rows = slice(c * L, (c + 1) * L)
            o_ref[0, rows, :] = o_intra[rows] + _dot_nt(qb[rows], _bf(st))
            st = st * jnp.exp(tot[c * L:c * L + 1]) + _dot_tn(v[rows], kt[rows])
        st_ref[d] = st


def gla_scan(z, gate_up_p, gate_b, tl):
    B, T, _ = z.shape
    H = GLA_HEADS
    nblk = T // tl
    qc = Z_SEGS["q"][2] // GLA_DK
    kc = Z_SEGS["k"][2] // GLA_DK
    vc = Z_SEGS["v"][2] // GLA_DV
    gc = Z_SEGS["gg"][2] // LANES
    specs = []
    for tmap in (lambda i: i, lambda i: nblk - 1 - i):
        specs += [
            pl.BlockSpec((1, tl, GLA_DK), lambda b, h, i, tmap=tmap: (b, tmap(i), qc + h)),
            pl.BlockSpec((1, tl, GLA_DK), lambda b, h, i, tmap=tmap: (b, tmap(i), kc + h)),
            pl.BlockSpec((1, tl, GLA_DV), lambda b, h, i, tmap=tmap: (b, tmap(i), vc + h)),
            pl.BlockSpec((1, tl, LANES), lambda b, h, i, tmap=tmap: (b, tmap(i), gc)),
        ]
    o_sds = jax.ShapeDtypeStruct((B, T, GLA_VDIM), F32)
    return pl.pallas_call(
        functools.partial(_gla_body, n_chunks=tl // CHUNK),
        grid=(B, H, nblk),
        in_specs=specs + [
            pl.BlockSpec((2, LANES, GLA_DK), lambda b, h, i: (0, 0, h)),
            pl.BlockSpec((2, 1, GLA_DK), lambda b, h, i: (0, 0, h)),
        ],
        out_specs=[pl.BlockSpec((1, tl, GLA_DV), lambda b, h, i: (b, i, h)),
                   pl.BlockSpec((1, tl, GLA_DV), lambda b, h, i: (b, nblk - 1 - i, h))],
        out_shape=[o_sds, o_sds],
        scratch_shapes=[pltpu.VMEM((2, GLA_DV, GLA_DK), F32)],
        compiler_params=_cparams(("parallel", "parallel", "arbitrary")),
        name="gla_scan",
    )(z, z, z, z, z, z, z, z, gate_up_p, gate_b.reshape(2, 1, GLA_KDIM))


def _gla_post_body(of_ref, ob_ref, og_ref, g_ref, y_ref):
    o = of_ref[0] + ob_ref[0]
    ms = jnp.mean(o * o, axis=-1, keepdims=True)
    on = o * lax.rsqrt(ms + LN_EPS) * g_ref[...]
    og = og_ref[0]
    y_ref[0] = (on * (og * _sigmoid(og))).astype(y_ref.dtype)


def gla_post(o_f, o_b, z, norm_g, tm):
    B, T, _ = o_f.shape
    H = GLA_HEADS
    oc = Z_SEGS["og"][2] // GLA_DV
    return pl.pallas_call(
        _gla_post_body,
        grid=(B, T // tm, H),
        in_specs=[
            pl.BlockSpec((1, tm, GLA_DV), lambda b, i, h: (b, i, h)),
            pl.BlockSpec((1, tm, GLA_DV), lambda b, i, h: (b, i, h)),
            pl.BlockSpec((1, tm, GLA_DV), lambda b, i, h: (b, i, oc + h)),
            pl.BlockSpec((1, GLA_DV), lambda b, i, h: (0, 0)),
        ],
        out_specs=pl.BlockSpec((1, tm, GLA_DV), lambda b, i, h: (b, i, h)),
        out_shape=jax.ShapeDtypeStruct((B, T, GLA_VDIM), BF16),
        compiler_params=_cparams(("parallel", "parallel", "parallel")),
        name="gla_post",
    )(o_f, o_b, z, norm_g.reshape(1, GLA_DV))


def _pad_ab_columns(w):
    out = jnp.zeros((w.shape[0], Z_COLS), w.dtype)
    for o_start, width, n_start in Z_SEGS.values():
        out = lax.dynamic_update_slice(out, w[:, o_start:o_start + width], (0, n_start))
    return out


def _pad_rows(w, rows):
    pad = [(0, 0)] * w.ndim
    pad[-2] = (0, rows - w.shape[-2])
    return jnp.pad(w, pad)


def _qk_perm():
    half = DIFF_HEAD_DIM // 2
    perm = np.zeros(D_MODEL, np.int32)
    for h in range(DIFF_HEADS):
        for c in range(2):
            for j in range(DIFF_HEAD_DIM):
                p = (j // half) * DIFF_HEAD_DIM + c * half + j % half
                perm[h * DIFF_V_DIM + p] = h * DIFF_V_DIM + c * DIFF_HEAD_DIM + j
    return perm


def _rope_tables(T):
    half = DIFF_HEAD_DIM // 2
    inv = ROPE_THETA ** (-jnp.arange(0, DIFF_HEAD_DIM, 2, dtype=F32) / DIFF_HEAD_DIM)
    ang = jnp.arange(T, dtype=F32)[:, None] * inv[None, :]
    cos, sin = jnp.cos(ang), jnp.sin(ang)
    cos_t = jnp.tile(cos, (1, 4))
    sin_t = jnp.concatenate([-sin, -sin, sin, sin], axis=1)
    return cos_t, sin_t


def _lambda_init(layer):
    return 0.8 - 0.6 * math.exp(-0.3 * layer)


def _mixer_ab(x, shift, scale, ab_w_in, rwkv_mu, rwkv_w0, rwkv_w_up, rwkv_a0, rwkv_a_up,
              rwkv_g_up, rwkv_k_k, rwkv_k_a, rwkv_r_k, rwkv_gn_g, rwkv_gn_b,
              gla_gate_up, gla_gate_b, gla_norm_g, blocks):
    w_p = _pad_ab_columns(ab_w_in).astype(BF16)
    z = modulated_projection(x, shift, scale, w_p, F32, blocks["proj_tm"], 1024)

    mu_p = jnp.zeros((1, Z_A_COLS), F32)
    for name in ("rkv", "wd", "ad", "gd"):
        o_start, width, n_start = Z_SEGS[name]
        mu_p = lax.dynamic_update_slice(mu_p, rwkv_mu[None, o_start:o_start + width], (0, n_start))
    r, k, v, kk, ba, lw, g = rwkv_pre(
        z, mu_p, rwkv_w0, _pad_rows(rwkv_w_up, LORA_PAD), rwkv_a0, _pad_rows(rwkv_a_up, LORA_PAD),
        rwkv_g_up, rwkv_k_k, rwkv_k_a, blocks["pre_tm"])
    y_f, y_r = rwkv_scan(r, k, v, kk, ba, lw, blocks["scan_tl"])
    y_a = rwkv_post(y_f, y_r, r, k, v, g, rwkv_r_k, rwkv_gn_g, rwkv_gn_b, blocks["pre_tm"])

    o_f, o_r = gla_scan(z, _pad_rows(gla_gate_up, LANES), gla_gate_b, blocks["scan_tl"])
    y_b = gla_post(o_f, o_r, z, gla_norm_g, blocks["pre_tm"])
    return y_a, y_b


def _mixer_c(x, shift, scale, diff_w_in, diff_lambda, diff_subln_g, lambda_init, blocks):
    T = x.shape[1]
    perm = _qk_perm()
    w = jnp.concatenate([diff_w_in[:, :D_MODEL][:, perm],
                         diff_w_in[:, D_MODEL:2 * D_MODEL][:, perm],
                         diff_w_in[:, 2 * D_MODEL:]], axis=1).astype(BF16)
    cos_t, sin_t = _rope_tables(T)
    qkv = qkv_projection(x, shift, scale, w, cos_t, sin_t, blocks["proj_tm"], 1024)
    return diff_attention(qkv, diff_lambda, diff_subln_g, lambda_init, blocks["attn_tq"])


def _blocks(T):
    return {
        "proj_tm": min(1024, T),
        "pre_tm": min(256, T),
        "scan_tl": min(256, T),
        "attn_tq": min(256, T),
        "ln_tm": min(512, T),
        "ffn_tm": min(512, T),
    }


def kernel(x, c, ada_w, ada_b, ln_g, ln_b, ffn_w_in, ffn_w_out, ab_w_in, ab_w_out, rwkv_mu, rwkv_w0, rwkv_w_up, rwkv_a0, rwkv_a_up, rwkv_g_up, rwkv_k_k, rwkv_k_a, rwkv_r_k, rwkv_gn_g, rwkv_gn_b, gla_gate_up, gla_gate_b, gla_norm_g, diff_w_in, diff_w_out, diff_lambda, diff_subln_g):
    blocks = _blocks(x.shape[1])
    shift, scale, gate = ada_modulation(c, ada_w, ada_b)
    for i in range(DEPTH):
        j = i // 2
        m = 2 * i
        if i % 2 == 0:
            y_parts = _mixer_ab(
                x, shift[m], scale[m], ab_w_in[j], rwkv_mu[j], rwkv_w0[j], rwkv_w_up[j],
                rwkv_a0[j], rwkv_a_up[j], rwkv_g_up[j], rwkv_k_k[j], rwkv_k_a[j], rwkv_r_k[j],
                rwkv_gn_g[j], rwkv_gn_b[j], gla_gate_up[j], gla_gate_b[j], gla_norm_g[j], blocks)
            w_out = ab_w_out[j]
        else:
            y_parts = (_mixer_c(x, shift[m], scale[m], diff_w_in[j], diff_lambda[j],
                                diff_subln_g[j], _lambda_init(i), blocks),)
            w_out = diff_w_out[j]
        x = projection_layernorm(y_parts, w_out.astype(BF16), x, gate[m], ln_g[i, 0], ln_b[i, 0],
                                 blocks["ln_tm"])
        x = ffn_sublayer(x, shift[m + 1], scale[m + 1], gate[m + 1], ffn_w_in[i].astype(BF16),
                         ffn_w_out[i].astype(BF16), ln_g[i, 1], ln_b[i, 1], blocks["ffn_tm"], 512)
    return x
```

```python
import functools
import math

import numpy as np
import jax
import jax.numpy as jnp
from jax import lax
from jax.experimental import pallas as pl
from jax.experimental.pallas import tpu as pltpu

F32 = jnp.float32
BF16 = jnp.bfloat16
HI = lax.Precision.HIGHEST

D_MODEL = 2048
DEPTH = 2
ALPHA = (2.0 * DEPTH) ** 0.25
LN_EPS = 1e-5

RWKV_HEAD_DIM = 64
RWKV_DIM = D_MODEL // 2
RWKV_HEADS = RWKV_DIM // RWKV_HEAD_DIM
DECAY_LORA = 96
ICL_LORA = 96
GATE_LORA = 256
W_DECAY_SCALE = 0.606531
RWKV_GN_EPS = 64e-5

GLA_HEADS = 4
GLA_VDIM = D_MODEL // 2
GLA_KDIM = GLA_VDIM // 2
GLA_DK = GLA_KDIM // GLA_HEADS
GLA_DV = GLA_VDIM // GLA_HEADS
GLA_GATE_LORA = 16
GLA_GATE_NORMALIZER = 16.0

DIFF_HEAD_DIM = 64
DIFF_V_DIM = 2 * DIFF_HEAD_DIM
DIFF_HEADS = D_MODEL // DIFF_V_DIM
ROPE_THETA = 10000.0

D_FF = -(-(8 * D_MODEL) // (3 * 256)) * 256

LANES = 128
VMEM_LIMIT = 56 * 1024 * 1024

CHUNK = 64
LORA_PAD = 128
HALO_ROWS = 16
Z_SEGS = {
    "rkv": (0, 3 * RWKV_DIM, 0),
    "wd": (3 * RWKV_DIM, DECAY_LORA, 3072),
    "ad": (3 * RWKV_DIM + DECAY_LORA, ICL_LORA, 3200),
    "gd": (3 * RWKV_DIM + DECAY_LORA + ICL_LORA, GATE_LORA, 3328),
    "q": (3520, GLA_KDIM, 3584),
    "k": (3520 + GLA_KDIM, GLA_KDIM, 4096),
    "v": (3520 + 2 * GLA_KDIM, GLA_VDIM, 4608),
    "gg": (3520 + 2 * GLA_KDIM + GLA_VDIM, GLA_GATE_LORA, 6656),
    "og": (3520 + 2 * GLA_KDIM + GLA_VDIM + GLA_GATE_LORA, GLA_VDIM, 5632),
}
Z_A_COLS = 3584
Z_COLS = 7168


def _cparams(sem):
    return pltpu.CompilerParams(dimension_semantics=sem, vmem_limit_bytes=VMEM_LIMIT)


def _dot(a, b, prec=None):
    return jnp.dot(a, b, preferred_element_type=F32, precision=prec)


def _dot_nt(a, b, prec=None):
    return lax.dot_general(a, b, (((1,), (1,)), ((), ())), preferred_element_type=F32,
                           precision=prec)


def _dot_tn(a, b, prec=None):
    return lax.dot_general(a, b, (((0,), (0,)), ((), ())), preferred_element_type=F32,
                           precision=prec)


def _sigmoid(x):
    return 1.0 / (1.0 + jnp.exp(-x))


def _layer_norm_rows(u, g, b, eps):
    mu = jnp.mean(u, axis=-1, keepdims=True)
    d = u - mu
    var = jnp.mean(d * d, axis=-1, keepdims=True)
    return d * lax.rsqrt(var + eps) * g + b


def _ada_body(c_ref, w_ref, b_ref, o_ref):
    c = c_ref[...]
    sc = (c * _sigmoid(c)).astype(BF16)
    o_ref[0] = _dot(sc, w_ref[0].astype(BF16)) + b_ref[0]


def ada_modulation(c, ada_w, ada_b):
    B, D = c.shape
    n = ada_w.shape[0] * ada_w.shape[1]
    w = ada_w.reshape(n, D, 3 * D)
    b = ada_b.reshape(n, 1, 3 * D)
    rows = 8
    c_pad = jnp.pad(c, ((0, rows - B), (0, 0)))
    tn = 1024
    out = pl.pallas_call(
        _ada_body,
        grid=(n, 3 * D // tn),
        in_specs=[
            pl.BlockSpec((rows, D), lambda i, j: (0, 0)),
            pl.BlockSpec((1, D, tn), lambda i, j: (i, 0, j)),
            pl.BlockSpec((1, 1, tn), lambda i, j: (i, 0, j)),
        ],
        out_specs=pl.BlockSpec((1, rows, tn), lambda i, j: (i, 0, j)),
        out_shape=jax.ShapeDtypeStruct((n, rows, 3 * D), F32),
        compiler_params=_cparams(("parallel", "parallel")),
        name="ada_modulation",
    )(c_pad, w, b)
    mods = out[:, :B, :]
    shift, scale, gate = mods[..., :D], mods[..., D:2 * D], mods[..., 2 * D:]
    r3 = lambda t: t.reshape(n, B, 1, D)
    return r3(shift), r3(scale), r3(gate)


def _inproj_body(x_ref, sh_ref, sc_ref, w_ref, o_ref, h_ref):
    @pl.when(pl.program_id(2) == 0)
    def _():
        h_ref[...] = (x_ref[0] * (1.0 + sc_ref[0]) + sh_ref[0]).astype(BF16)

    o_ref[0] = _dot(h_ref[...], w_ref[...]).astype(o_ref.dtype)


def modulated_projection(x, shift, scale, w, out_dtype, tm, tn):
    B, T, D = x.shape
    N = w.shape[1]
    return pl.pallas_call(
        _inproj_body,
        grid=(B, T // tm, N // tn),
        in_specs=[
            pl.BlockSpec((1, tm, D), lambda b, i, j: (b, i, 0)),
            pl.BlockSpec((1, 1, D), lambda b, i, j: (b, 0, 0)),
            pl.BlockSpec((1, 1, D), lambda b, i, j: (b, 0, 0)),
            pl.BlockSpec((D, tn), lambda b, i, j: (0, j)),
        ],
        out_specs=pl.BlockSpec((1, tm, tn), lambda b, i, j: (b, i, j)),
        out_shape=jax.ShapeDtypeStruct((B, T, N), out_dtype),
        scratch_shapes=[pltpu.VMEM((tm, D), BF16)],
        compiler_params=_cparams(("parallel", "parallel", "arbitrary")),
        name="modulated_projection",
    )(x, shift, scale, w)


def _qkv_body(x_ref, sh_ref, sc_ref, w_ref, cos_ref, sin_ref, o_ref, h_ref, *, n_q, n_qk,
              q_scale):
    j = pl.program_id(2)

    @pl.when(j == 0)
    def _():
        h_ref[...] = (x_ref[0] * (1.0 + sc_ref[0]) + sh_ref[0]).astype(BF16)

    acc = _dot(h_ref[...], w_ref[...])
    tn = acc.shape[1]

    @pl.when(j < n_qk)
    def _():
        cos = cos_ref[...]
        sin = sin_ref[...]
        mult = jnp.where(j < n_q, q_scale, 1.0).astype(F32)
        for s in range(tn // LANES):
            xs = acc[:, s * LANES:(s + 1) * LANES]
            rot = pltpu.roll(xs, LANES // 2, axis=1)
            o_ref[0, :, s * LANES:(s + 1) * LANES] = ((xs * cos + rot * sin) * mult).astype(o_ref.dtype)

    @pl.when(j >= n_qk)
    def _():
        o_ref[0] = acc.astype(o_ref.dtype)


def qkv_projection(x, shift, scale, w, cos_t, sin_t, tm, tn):
    B, T, D = x.shape
    N = w.shape[1]
    n_q = D_MODEL // tn
    body = functools.partial(_qkv_body, n_q=n_q, n_qk=2 * n_q,
                             q_scale=DIFF_HEAD_DIM ** -0.5 * math.log2(math.e))
    return pl.pallas_call(
        body,
        grid=(B, T // tm, N // tn),
        in_specs=[
            pl.BlockSpec((1, tm, D), lambda b, i, j: (b, i, 0)),
            pl.BlockSpec((1, 1, D), lambda b, i, j: (b, 0, 0)),
            pl.BlockSpec((1, 1, D), lambda b, i, j: (b, 0, 0)),
            pl.BlockSpec((D, tn), lambda b, i, j: (0, j)),
            pl.BlockSpec((tm, LANES), lambda b, i, j: (i, 0)),
            pl.BlockSpec((tm, LANES), lambda b, i, j: (i, 0)),
        ],
        out_specs=pl.BlockSpec((1, tm, tn), lambda b, i, j: (b, i, j)),
        out_shape=jax.ShapeDtypeStruct((B, T, N), BF16),
        scratch_shapes=[pltpu.VMEM((tm, D), BF16)],
        compiler_params=_cparams(("parallel", "parallel", "arbitrary")),
        name="qkv_projection",
    )(x, shift, scale, w, cos_t, sin_t)


def _proj_ln_body(*refs, n_parts):
    y_refs = refs[:n_parts]
    w_ref, x_ref, gate_ref, g_ref, b_ref, o_ref = refs[n_parts:]
    acc = None
    off = 0
    for yr in y_refs:
        k = yr.shape[-1]
        part = _dot(yr[0], w_ref[off:off + k, :])
        acc = part if acc is None else acc + part
        off += k
    u = ALPHA * x_ref[0] + (1.0 + gate_ref[0]) * acc
    o_ref[0] = _layer_norm_rows(u, g_ref[...], b_ref[...], LN_EPS)


def projection_layernorm(y_parts, w, x, gate, ln_g, ln_b, tm):
    B, T, D = x.shape
    K = w.shape[0]
    in_specs = [pl.BlockSpec((1, tm, yp.shape[-1]), lambda b, i: (b, i, 0)) for yp in y_parts]
    in_specs += [
        pl.BlockSpec((K, D), lambda b, i: (0, 0)),
        pl.BlockSpec((1, tm, D), lambda b, i: (b, i, 0)),
        pl.BlockSpec((1, 1, D), lambda b, i: (b, 0, 0)),
        pl.BlockSpec((1, D), lambda b, i: (0, 0)),
        pl.BlockSpec((1, D), lambda b, i: (0, 0)),
    ]
    return pl.pallas_call(
        functools.partial(_proj_ln_body, n_parts=len(y_parts)),
        grid=(B, T // tm),
        in_specs=in_specs,
        out_specs=pl.BlockSpec((1, tm, D), lambda b, i: (b, i, 0)),
        out_shape=jax.ShapeDtypeStruct((B, T, D), F32),
        compiler_params=_cparams(("parallel", "parallel")),
        name="projection_layernorm",
    )(*y_parts, w, x, gate, ln_g.reshape(1, D), ln_b.reshape(1, D))


def _ffn_body(x_ref, sh_ref, sc_ref, gate_ref, wg_ref, wu_ref, wo_ref, g_ref, b_ref, o_ref,
              h_ref, acc_ref):
    j = pl.program_id(2)

    @pl.when(j == 0)
    def _():
        h_ref[...] = (x_ref[0] * (1.0 + sc_ref[0]) + sh_ref[0]).astype(BF16)
        acc_ref[...] = jnp.zeros_like(acc_ref)

    h = h_ref[...]
    gt = _dot(h, wg_ref[0])
    up = _dot(h, wu_ref[0])
    act = (gt * _sigmoid(gt) * up).astype(BF16)
    acc_ref[...] += _dot(act, wo_ref[0])

    @pl.when(j == pl.num_programs(2) - 1)
    def _():
        u = ALPHA * x_ref[0] + (1.0 + gate_ref[0]) * acc_ref[...]
        o_ref[0] = _layer_norm_rows(u, g_ref[...], b_ref[...], LN_EPS)


def ffn_sublayer(x, shift, scale, gate, w_in, w_out, layer, ln_g, ln_b, tm, tf):
    B, T, D = x.shape
    F = w_out.shape[1]
    nf = F // tf
    return pl.pallas_call(
        _ffn_body,
        grid=(B, T // tm, nf),
        in_specs=[
            pl.BlockSpec((1, tm, D), lambda b, i, j: (b, i, 0)),
            pl.BlockSpec((1, 1, D), lambda b, i, j: (b, 0, 0)),
            pl.BlockSpec((1, 1, D), lambda b, i, j: (b, 0, 0)),
            pl.BlockSpec((1, 1, D), lambda b, i, j: (b, 0, 0)),
            pl.BlockSpec((1, D, tf), lambda b, i, j: (layer, 0, j)),
            pl.BlockSpec((1, D, tf), lambda b, i, j: (layer, 0, j + nf)),
            pl.BlockSpec((1, tf, D), lambda b, i, j: (layer, j, 0)),
            pl.BlockSpec((1, D), lambda b, i, j: (0, 0)),
            pl.BlockSpec((1, D), lambda b, i, j: (0, 0)),
        ],
        out_specs=pl.BlockSpec((1, tm, D), lambda b, i, j: (b, i, 0)),
        out_shape=jax.ShapeDtypeStruct((B, T, D), F32),
        scratch_shapes=[pltpu.VMEM((tm, D), BF16), pltpu.VMEM((tm, D), F32)],
        compiler_params=_cparams(("parallel", "parallel", "arbitrary")),
        name="ffn_sublayer",
    )(x, shift, scale, gate, w_in, w_in, w_out, ln_g.reshape(1, D), ln_b.reshape(1, D))


ONES_ROWS = 16


KEY_CHUNK = 512


def _attn_body(q0_ref, qn_ref, k_ref, v_ref, lam_ref, g_ref, o_ref, vt_ref, s_ref, m_ref,
               *, lambda_init):
    dv = DIFF_V_DIM
    T = k_ref.shape[1]
    lane = lax.broadcasted_iota(jnp.int32, (1, LANES), 1)
    comp0 = (lane // (DIFF_HEAD_DIM // 2)) % 2 == 0

    def comps(q):
        zero = jnp.zeros_like(q)
        return (jnp.where(comp0, q, zero), jnp.where(comp0, zero, q))

    def chunk_scores(qcs, rows):
        kc = k_ref[0, rows, :]
        return [_dot_nt(kc, qc) for qc in qcs]

    def col_max(s):
        return jnp.max(s, axis=0, keepdims=True)

    chunks = [slice(c * KEY_CHUNK, (c + 1) * KEY_CHUNK) for c in range(T // KEY_CHUNK)]

    @pl.when(pl.program_id(2) == 0)
    def _():
        vt_ref[0:dv, :] = v_ref[0].astype(F32).T.astype(BF16)
        vt_ref[dv:, :] = jnp.ones((ONES_ROWS, T), BF16)
        qcs = comps(q0_ref[0])
        m = [None, None]
        for rows in chunks:
            s = chunk_scores(qcs, rows)
            for j in range(2):
                s_ref[j, rows, :] = s[j]
                m[j] = col_max(s[j]) if m[j] is None else jnp.maximum(m[j], col_max(s[j]))
        for j in range(2):
            m_ref[j] = m[j]

    qcs = comps(qn_ref[0])
    m_cur = [m_ref[0], m_ref[1]]
    m_new = [None, None]
    acc = [None, None]
    for rows in chunks:
        s_new = chunk_scores(qcs, rows)
        for j in range(2):
            e = jnp.exp2(s_ref[j, rows, :] - m_cur[j]).astype(BF16)
            part = _dot(vt_ref[:, rows], e)
            acc[j] = part if acc[j] is None else acc[j] + part
        for j in range(2):
            s_ref[j, rows, :] = s_new[j]
            m_new[j] = col_max(s_new[j]) if m_new[j] is None else jnp.maximum(m_new[j], col_max(s_new[j]))
    for j in range(2):
        m_ref[j] = m_new[j]

    lp = lam_ref[...]
    lam = (jnp.exp(jnp.sum(lp[0:1] * lp[1:2], axis=-1, keepdims=True))
           - jnp.exp(jnp.sum(lp[2:3] * lp[3:4], axis=-1, keepdims=True)) + lambda_init)
    outs = [a[0:dv] / a[dv:dv + 1] for a in acc]
    o = (outs[0] - lam * outs[1]).T
    ms = jnp.mean(o * o, axis=-1, keepdims=True)
    o = o * lax.rsqrt(ms + LN_EPS) * g_ref[...] * (1.0 - lambda_init)
    o_ref[0] = o.astype(o_ref.dtype)


def diff_attention(qkv, lam_params, subln_g, lambda_init, tq):
    B, T, _ = qkv.shape
    H = DIFF_HEADS
    nq = T // tq
    return pl.pallas_call(
        functools.partial(_attn_body, lambda_init=lambda_init),
        grid=(B, H, nq),
        in_specs=[
            pl.BlockSpec((1, tq, LANES), lambda b, h, i: (b, 0, h)),
            pl.BlockSpec((1, tq, LANES), lambda b, h, i: (b, jnp.minimum(i + 1, nq - 1), h)),
            pl.BlockSpec((1, T, LANES), lambda b, h, i: (b, 0, H + h)),
            pl.BlockSpec((1, T, LANES), lambda b, h, i: (b, 0, 2 * H + h)),
            pl.BlockSpec((4, DIFF_HEAD_DIM), lambda b, h, i: (0, 0)),
            pl.BlockSpec((1, LANES), lambda b, h, i: (0, 0)),
        ],
        out_specs=pl.BlockSpec((1, tq, LANES), lambda b, h, i: (b, i, h)),
        out_shape=jax.ShapeDtypeStruct((B, T, H * DIFF_V_DIM), BF16),
        scratch_shapes=[pltpu.VMEM((DIFF_V_DIM + ONES_ROWS, T), BF16),
                        pltpu.VMEM((2, T, tq), F32),
                        pltpu.VMEM((2, 1, tq), F32)],
        compiler_params=_cparams(("parallel", "parallel", "arbitrary")),
        name="diff_attention",
    )(qkv, qkv, qkv, qkv, lam_params, subln_g.reshape(1, LANES))


def _per_head_sum(x, head_dim):
    lane = lax.broadcasted_iota(jnp.int32, (1, LANES), 1)
    lo = lane < head_dim
    s_lo = jnp.sum(jnp.where(lo, x, 0.0), axis=-1, keepdims=True)
    s_hi = jnp.sum(jnp.where(lo, 0.0, x), axis=-1, keepdims=True)
    return jnp.where(lo, s_lo, s_hi)


def _rwkv_pre_body(z_ref, zp_ref, zn_ref, mu_ref, w0_ref, wup_ref, a0_ref, aup_ref, gup_ref,
                   kk_ref, ka_ref, r_o, k_o, v_o, kk_o, ba_o, lw_o, g_o, *, tm):
    i = pl.program_id(1)
    last = pl.num_programs(1) - 1
    row = lax.broadcasted_iota(jnp.int32, (tm, 1), 0)
    C = RWKV_DIM

    def shifted(lo, hi):
        z = z_ref[0, :, lo:hi].astype(F32)
        prev = jnp.where(i == 0, 0.0, zp_ref[0, HALO_ROWS - 1:HALO_ROWS, lo:hi].astype(F32))
        nxt = jnp.where(i == last, 0.0, zn_ref[0, 0:1, lo:hi].astype(F32))
        z_dn = jnp.where(row == 0, prev, pltpu.roll(z, 1, axis=0))
        z_up = jnp.where(row == tm - 1, nxt, pltpu.roll(z, tm - 1, axis=0))
        return z + (0.5 * (z_dn + z_up) - z) * mu_ref[:, lo:hi]

    r = shifted(0, C)
    k = shifted(C, 2 * C)
    v = shifted(2 * C, 3 * C)
    wd = shifted(3 * C, 3 * C + LORA_PAD)
    ad = shifted(3 * C + LORA_PAD, 3 * C + 2 * LORA_PAD)
    gd = shifted(3 * C + 2 * LORA_PAD, 3 * C + 2 * LORA_PAD + GATE_LORA)

    r_o[0] = r.astype(r_o.dtype)
    v_o[0] = v.astype(v_o.dtype)
    twd = jnp.tanh(wd).astype(BF16)
    for d in range(2):
        lw_o[d, 0] = -W_DECAY_SCALE * _sigmoid(w0_ref[d] + _dot(twd, wup_ref[d].astype(BF16)))
    a = _sigmoid(a0_ref[...] + _dot(ad.astype(BF16), aup_ref[...].astype(BF16)))
    g_o[0] = _dot(_sigmoid(gd).astype(BF16), gup_ref[...].astype(BF16)).astype(g_o.dtype)
    k_o[0] = (k * (1.0 + (a - 1.0) * ka_ref[...])).astype(k_o.dtype)
    kk0 = k * kk_ref[...]
    for s in range(C // LANES):
        sl = slice(s * LANES, (s + 1) * LANES)
        x = kk0[:, sl]
        nrm = jnp.maximum(jnp.sqrt(_per_head_sum(x * x, RWKV_HEAD_DIM)), 1e-12)
        kk = x / nrm
        kk_o[0, :, sl] = kk.astype(kk_o.dtype)
        ba_o[0, :, sl] = (kk * a[:, sl]).astype(ba_o.dtype)


def rwkv_pre(z, mu_p, w0, wup_p, a0, aup_p, gup, k_k, k_a, tm):
    B, T, _ = z.shape
    C = RWKV_DIM
    nbh = tm // HALO_ROWS
    nh = T // HALO_ROWS
    bt = jax.ShapeDtypeStruct((B, T, C), BF16)
    vec = lambda t: t.reshape(1, C)
    full = lambda shp: pl.BlockSpec(shp, lambda b, i: (0,) * len(shp))
    body = functools.partial(_rwkv_pre_body, tm=tm)
    return pl.pallas_call(
        body,
        grid=(B, T // tm),
        in_specs=[
            pl.BlockSpec((1, tm, Z_A_COLS), lambda b, i: (b, i, 0)),
            pl.BlockSpec((1, HALO_ROWS, Z_A_COLS), lambda b, i: (b, jnp.maximum(i * nbh - 1, 0), 0)),
            pl.BlockSpec((1, HALO_ROWS, Z_A_COLS),
                         lambda b, i: (b, jnp.minimum((i + 1) * nbh, nh - 1), 0)),
            full((1, Z_A_COLS)),
            full((2, 1, C)),
            full((2, LORA_PAD, C)),
            full((1, C)),
            full((LORA_PAD, C)),
            full((GATE_LORA, C)),
            full((1, C)),
            full((1, C)),
        ],
        out_specs=[pl.BlockSpec((1, tm, C), lambda b, i: (b, i, 0))] * 5
        + [pl.BlockSpec((2, 1, tm, C), lambda b, i: (0, b, i, 0)),
           pl.BlockSpec((1, tm, C), lambda b, i: (b, i, 0))],
        out_shape=[bt] * 5 + [jax.ShapeDtypeStruct((2, B, T, C), F32), bt],
        compiler_params=_cparams(("parallel", "parallel")),
        name="rwkv_pre",
    )(z, z, z, mu_p, w0.reshape(2, 1, C), wup_p, vec(a0), aup_p, gup, vec(k_k), vec(k_a))


def _bf(x):
    return x.astype(BF16)


def _segmented_cumsum(x, seg):
    pos = lax.broadcasted_iota(jnp.int32, (x.shape[0], 1), 0) % seg
    s = 1
    while s < seg:
        x = x + jnp.where(pos >= s, pltpu.roll(x, s, axis=0), 0.0)
        s *= 2
    return x


def _segment_totals(x, seg):
    parts = []
    for c in range(x.shape[0] // seg):
        t = jnp.sum(x[c * seg:(c + 1) * seg], axis=0, keepdims=True)
        parts.append(jnp.broadcast_to(t, (seg, x.shape[1])))
    return jnp.concatenate(parts, axis=0)


def _decay_factors(lw, reverse):
    tot = _segment_totals(lw, CHUNK)
    cum = _segmented_cumsum(lw, CHUNK)
    if reverse:
        cum = tot - cum + lw
    return cum, tot


def _rwkv_chunks_local(chunks):
    L = CHUNK
    lane = lax.broadcasted_iota(jnp.int32, (1, LANES), 1)
    h0 = lane < RWKV_HEAD_DIM

    def stack(x):
        return jnp.concatenate([jnp.where(h0, x, 0.0), jnp.where(h0, 0.0, x)], axis=0)

    pre = []
    for r, k, v, kk, ba, lw, cum, tot, strict, incl in chunks:
        g_ex = jnp.exp(cum - lw)
        g_inv = jnp.exp(-cum)
        g_rem = jnp.exp(tot - cum)
        a_st = stack(-kk * g_ex)
        r_st = stack(r * jnp.exp(cum))
        lhs = _bf(jnp.concatenate([a_st, r_st], axis=0))
        rhs = _bf(jnp.concatenate([stack(ba * g_inv), stack(k * g_inv)], axis=0))
        t_b = _bf(jnp.concatenate([stack(ba * g_rem), stack(k * g_rem)], axis=0))
        pre.append((a_st, r_st, lhs, rhs, t_b, _bf(stack(v))))

    ps = [_dot_nt(lhs, rhs) for _, _, lhs, rhs, _, _ in pre]
    m_ab, m_ak, m_r = [], [], []
    for p, ch in zip(ps, chunks):
        strict, incl = ch[8], ch[9]
        m_ab.append(jnp.where(strict, p[:2 * L, :2 * L], 0.0))
        m_ak.append(_bf(jnp.where(strict, p[:2 * L, 2 * L:], 0.0)))
        m_r.append(_bf(jnp.concatenate([jnp.where(incl, p[2 * L:, :2 * L], 0.0),
                                        jnp.where(incl, p[2 * L:, 2 * L:], 0.0)], axis=1)))

    akv = [_dot(m, pr[5]) for m, pr in zip(m_ak, pre)]
    xs = [jnp.concatenate([pr[0], t], axis=1) for pr, t in zip(pre, akv)]
    ns = m_ab
    steps = int(math.log2(L))
    for it in range(steps):
        n_bs = [_bf(n) for n in ns]
        xs = [x + _dot(n_b, _bf(x)) for x, n_b in zip(xs, n_bs)]
        if it + 1 < steps:
            ns = [_dot(n_b, n_b) for n_b in n_bs]

    out = []
    wu = [(_bf(x[:, :LANES]), _bf(x[:, LANES:])) for x in xs]
    qys = [_dot(m, jnp.concatenate([jnp.concatenate([w_b, ul_b], axis=1),
                                    jnp.concatenate([jnp.zeros_like(pr[5]), pr[5]], axis=1)], axis=0))
           for m, (w_b, ul_b), pr in zip(m_r, wu, pre)]
    gps = [_dot_tn(w_b, pr[4][:2 * L]) for (w_b, _), pr in zip(wu, pre)]
    hs = [_dot_tn(jnp.concatenate([ul_b, pr[5]], axis=0), pr[4])
          for (_, ul_b), pr in zip(wu, pre)]
    for qy, gp, h, pr in zip(qys, gps, hs, pre):
        out.append((_bf(pr[1] + qy[:, :LANES]), qy[:, LANES:], _bf(gp), h))
    return out


def _rwkv_scan_body(rf, kf, vf, kkf, baf, lwf, rb, kb, vb, kkb, bab, lwb, yf_ref, yb_ref, s_ref,
                    *, n_chunks):
    L = CHUNK

    @pl.when(pl.program_id(2) == 0)
    def _():
        s_ref[...] = jnp.zeros_like(s_ref)

    r2 = lax.broadcasted_iota(jnp.int32, (2 * L, 2 * L), 0)
    c2 = lax.broadcasted_iota(jnp.int32, (2 * L, 2 * L), 1)
    same_head = (r2 // L) == (c2 // L)
    dt = r2 % L - c2 % L

    chunks, g_tots = [], []
    for d, refs in ((0, (rf, kf, vf, kkf, baf, lwf)), (1, (rb, kb, vb, kkb, bab, lwb))):
        r_ref, k_ref, v_ref, kk_ref, ba_ref, lw_ref = refs
        strict = jnp.where(same_head, dt if d == 0 else -dt, -1) > 0
        incl = strict | (r2 == c2)
        lw_all = lw_ref[0, 0]
        cum_all, tot_all = _decay_factors(lw_all, reverse=(d == 1))
        g_tot_all = jnp.exp(tot_all)
        for c in range(n_chunks):
            rows = slice(c * L, (c + 1) * L)
            chunks.append((r_ref[0, rows, :], k_ref[0, rows, :], v_ref[0, rows, :],
                           kk_ref[0, rows, :], ba_ref[0, rows, :], lw_all[rows], cum_all[rows],
                           tot_all[rows], strict, incl))
            g_tots.append(g_tot_all[c * L:c * L + 1])
    local = _rwkv_chunks_local(chunks)

    s2 = [s_ref[0], s_ref[1]]
    for j in range(n_chunks):
        for d, y_ref in ((0, yf_ref), (1, yb_ref)):
            c = j if d == 0 else n_chunks - 1 - j
            q_b, yl_st, gp_b, h = local[d * n_chunks + c]
            s_b = _bf(s2[d])
            y_st = _dot_nt(q_b, s_b) + yl_st
            y_ref[0, c * L:(c + 1) * L, :] = y_st[:L] + y_st[L:]
            s2[d] = s2[d] * g_tots[d * n_chunks + c] + _dot(s_b, gp_b) + h
    s_ref[0] = s2[0]
    s_ref[1] = s2[1]


def rwkv_scan(r, k, v, kk, ba, lw, tl):
    B, T, C = r.shape
    n_pairs = C // LANES
    nblk = T // tl
    fwd = pl.BlockSpec((1, tl, LANES), lambda b, p, i: (b, i, p))
    bwd = pl.BlockSpec((1, tl, LANES), lambda b, p, i: (b, nblk - 1 - i, p))
    lw_f = pl.BlockSpec((1, 1, tl, LANES), lambda b, p, i: (0, b, i, p))
    lw_b = pl.BlockSpec((1, 1, tl, LANES), lambda b, p, i: (1, b, nblk - 1 - i, p))
    y_sds = jax.ShapeDtypeStruct((B, T, C), F32)
    return pl.pallas_call(
        functools.partial(_rwkv_scan_body, n_chunks=tl // CHUNK),
        grid=(B, n_pairs, nblk),
        in_specs=[fwd] * 5 + [lw_f] + [bwd] * 5 + [lw_b],
        out_specs=[fwd, bwd],
        out_shape=[y_sds, y_sds],
        scratch_shapes=[pltpu.VMEM((2, LANES, LANES), F32)],
        compiler_params=_cparams(("parallel", "parallel", "arbitrary")),
        name="rwkv_scan",
    )(r, k, v, kk, ba, lw, r, k, v, kk, ba, lw)


def _rwkv_post_body(yf_ref, yb_ref, r_ref, k_ref, v_ref, g_ref, rk_ref, gg_ref, gb_ref, o_ref):
    C = RWKV_DIM
    for s in range(C // LANES):
        sl = slice(s * LANES, (s + 1) * LANES)
        y = yf_ref[0, :, sl] + yb_ref[0, :, sl]
        inv_n = 1.0 / RWKV_HEAD_DIM
        mu = _per_head_sum(y, RWKV_HEAD_DIM) * inv_n
        dlt = y - mu
        var = _per_head_sum(dlt * dlt, RWKV_HEAD_DIM) * inv_n
        yn = dlt * lax.rsqrt(var + RWKV_GN_EPS) * gg_ref[:, sl] + gb_ref[:, sl]
        rk = r_ref[0, :, sl].astype(F32) * k_ref[0, :, sl].astype(F32)
        bonus = _per_head_sum(rk * rk_ref[:, sl], RWKV_HEAD_DIM)
        out = (yn + bonus * v_ref[0, :, sl].astype(F32)) * g_ref[0, :, sl].astype(F32)
        o_ref[0, :, sl] = out.astype(o_ref.dtype)


def rwkv_post(yf, yb, r, k, v, g, r_k, gn_g, gn_b, tm):
    B, T, C = r.shape
    spec = pl.BlockSpec((1, tm, C), lambda b, i: (b, i, 0))
    vspec = pl.BlockSpec((1, C), lambda b, i: (0, 0))
    return pl.pallas_call(
        _rwkv_post_body,
        grid=(B, T // tm),
        in_specs=[spec] * 6 + [vspec] * 3,
        out_specs=spec,
        out_shape=jax.ShapeDtypeStruct((B, T, C), BF16),
        compiler_params=_cparams(("parallel", "parallel")),
        name="rwkv_post",
    )(yf, yb, r, k, v, g, r_k.reshape(1, C), gn_g.reshape(1, C), gn_b.reshape(1, C))


def _log_sigmoid(x):
    return jnp.minimum(x, 0.0) - jnp.log(1.0 + jnp.exp(-jnp.abs(x)))


def _gla_body(qf, kf, vf, ggf, qb_ref, kb_ref, vb_ref, ggb, gup_ref, gb_ref, of_ref, ob_ref, st_ref,
              *, n_chunks):
    L = CHUNK
    tl = n_chunks * L

    @pl.when(pl.program_id(2) == 0)
    def _():
        st_ref[...] = jnp.zeros_like(st_ref)

    ri = lax.broadcasted_iota(jnp.int32, (tl, tl), 0)
    ci = lax.broadcasted_iota(jnp.int32, (tl, tl), 1)
    same_chunk = (ri // L) == (ci // L)

    for d, refs, o_ref in ((0, (qf, kf, vf, ggf), of_ref), (1, (qb_ref, kb_ref, vb_ref, ggb), ob_ref)):
        q_ref, k_ref, v_ref, gg_ref = refs
        incl = jnp.where(same_chunk, (ri - ci) if d == 0 else (ci - ri), -1) >= 0
        lg = _log_sigmoid(_dot(_bf(gg_ref[0]), _bf(gup_ref[d])) + gb_ref[d]) * (1.0 / GLA_GATE_NORMALIZER)
        cum, tot = _decay_factors(lg, reverse=(d == 1))
        k = k_ref[0].astype(F32)
        v = _bf(v_ref[0])
        qb = _bf(q_ref[0].astype(F32) * (GLA_DK ** -0.5) * jnp.exp(cum))
        kb = _bf(k * jnp.exp(-cum))
        kt = _bf(k * jnp.exp(tot - cum))
        att = jnp.where(incl, _dot_nt(qb, kb), 0.0)
        o_intra = _dot(_bf(att), v)
        st = st_ref[d]
        order = range(n_chunks) if d == 0 else reversed(range(n_chunks))
        for c in order:
            rows = slice(c * L, (c + 1) * L)
            o_ref[0, rows, :] = o_intra[rows] + _dot_nt(qb[rows], _bf(st))
            st = st * jnp.exp(tot[c * L:c * L + 1]) + _dot_tn(v[rows], kt[rows])
        st_ref[d] = st


def gla_scan(z, gate_up_p, gate_b, tl):
    B, T, _ = z.shape
    H = GLA_HEADS
    nblk = T // tl
    qc = Z_SEGS["q"][2] // GLA_DK
    kc = Z_SEGS["k"][2] // GLA_DK
    vc = Z_SEGS["v"][2] // GLA_DV
    gc = Z_SEGS["gg"][2] // LANES
    specs = []
    for tmap in (lambda i: i, lambda i: nblk - 1 - i):
        specs += [
            pl.BlockSpec((1, tl, GLA_DK), lambda b, h, i, tmap=tmap: (b, tmap(i), qc + h)),
            pl.BlockSpec((1, tl, GLA_DK), lambda b, h, i, tmap=tmap: (b, tmap(i), kc + h)),
            pl.BlockSpec((1, tl, GLA_DV), lambda b, h, i, tmap=tmap: (b, tmap(i), vc + h)),
            pl.BlockSpec((1, tl, LANES), lambda b, h, i, tmap=tmap: (b, tmap(i), gc)),
        ]
    o_sds = jax.ShapeDtypeStruct((B, T, GLA_VDIM), F32)
    return pl.pallas_call(
        functools.partial(_gla_body, n_chunks=tl // CHUNK),
        grid=(B, H, nblk),
        in_specs=specs + [
            pl.BlockSpec((2, LANES, GLA_DK), lambda b, h, i: (0, 0, h)),
            pl.BlockSpec((2, 1, GLA_DK), lambda b, h, i: (0, 0, h)),
        ],
        out_specs=[pl.BlockSpec((1, tl, GLA_DV), lambda b, h, i: (b, i, h)),
                   pl.BlockSpec((1, tl, GLA_DV), lambda b, h, i: (b, nblk - 1 - i, h))],
        out_shape=[o_sds, o_sds],
        scratch_shapes=[pltpu.VMEM((2, GLA_DV, GLA_DK), F32)],
        compiler_params=_cparams(("parallel", "parallel", "arbitrary")),
        name="gla_scan",
    )(z, z, z, z, z, z, z, z, gate_up_p, gate_b.reshape(2, 1, GLA_KDIM))


def _gla_post_body(of_ref, ob_ref, og_ref, g_ref, y_ref):
    o = of_ref[0] + ob_ref[0]
    ms = jnp.mean(o * o, axis=-1, keepdims=True)
    on = o * lax.rsqrt(ms + LN_EPS) * g_ref[...]
    og = og_ref[0].astype(F32)
    y_ref[0] = (on * (og * _sigmoid(og))).astype(y_ref.dtype)


def gla_post(o_f, o_b, z, norm_g, tm):
    B, T, _ = o_f.shape
    H = GLA_HEADS
    oc = Z_SEGS["og"][2] // GLA_DV
    return pl.pallas_call(
        _gla_post_body,
        grid=(B, T // tm, H),
        in_specs=[
            pl.BlockSpec((1, tm, GLA_DV), lambda b, i, h: (b, i, h)),
            pl.BlockSpec((1, tm, GLA_DV), lambda b, i, h: (b, i, h)),
            pl.BlockSpec((1, tm, GLA_DV), lambda b, i, h: (b, i, oc + h)),
            pl.BlockSpec((1, GLA_DV), lambda b, i, h: (0, 0)),
        ],
        out_specs=pl.BlockSpec((1, tm, GLA_DV), lambda b, i, h: (b, i, h)),
        out_shape=jax.ShapeDtypeStruct((B, T, GLA_VDIM), BF16),
        compiler_params=_cparams(("parallel", "parallel", "parallel")),
        name="gla_post",
    )(o_f, o_b, z, norm_g.reshape(1, GLA_DV))


def _pad_ab_columns(w):
    out = jnp.zeros((w.shape[0], Z_COLS), w.dtype)
    for o_start, width, n_start in Z_SEGS.values():
        out = lax.dynamic_update_slice(out, w[:, o_start:o_start + width], (0, n_start))
    return out


def _pad_rows(w, rows):
    pad = [(0, 0)] * w.ndim
    pad[-2] = (0, rows - w.shape[-2])
    return jnp.pad(w, pad)


def _qk_perm():
    half = DIFF_HEAD_DIM // 2
    perm = np.zeros(D_MODEL, np.int32)
    for h in range(DIFF_HEADS):
        for c in range(2):
            for j in range(DIFF_HEAD_DIM):
                p = (j // half) * DIFF_HEAD_DIM + c * half + j % half
                perm[h * DIFF_V_DIM + p] = h * DIFF_V_DIM + c * DIFF_HEAD_DIM + j
    return perm


def _rope_tables(T):
    half = DIFF_HEAD_DIM // 2
    inv = ROPE_THETA ** (-jnp.arange(0, DIFF_HEAD_DIM, 2, dtype=F32) / DIFF_HEAD_DIM)
    ang = jnp.arange(T, dtype=F32)[:, None] * inv[None, :]
    cos, sin = jnp.cos(ang), jnp.sin(ang)
    cos_t = jnp.tile(cos, (1, 4))
    sin_t = jnp.concatenate([-sin, -sin, sin, sin], axis=1)
    return cos_t, sin_t


def _lambda_init(layer):
    return 0.8 - 0.6 * math.exp(-0.3 * layer)


def _mixer_ab(x, shift, scale, ab_w_in, rwkv_mu, rwkv_w0, rwkv_w_up, rwkv_a0, rwkv_a_up,
              rwkv_g_up, rwkv_k_k, rwkv_k_a, rwkv_r_k, rwkv_gn_g, rwkv_gn_b,
              gla_gate_up, gla_gate_b, gla_norm_g, blocks):
    w_p = _pad_ab_columns(ab_w_in).astype(BF16)
    z = modulated_projection(x, shift, scale, w_p, BF16, blocks["proj_tm"], 1024)

    mu_p = jnp.zeros((1, Z_A_COLS), F32)
    for name in ("rkv", "wd", "ad", "gd"):
        o_start, width, n_start = Z_SEGS[name]
        mu_p = lax.dynamic_update_slice(mu_p, rwkv_mu[None, o_start:o_start + width], (0, n_start))
    r, k, v, kk, ba, lw, g = rwkv_pre(
        z, mu_p, rwkv_w0, _pad_rows(rwkv_w_up, LORA_PAD), rwkv_a0, _pad_rows(rwkv_a_up, LORA_PAD),
        rwkv_g_up, rwkv_k_k, rwkv_k_a, blocks["pre_tm"])
    y_f, y_r = rwkv_scan(r, k, v, kk, ba, lw, blocks["scan_tl"])
    y_a = rwkv_post(y_f, y_r, r, k, v, g, rwkv_r_k, rwkv_gn_g, rwkv_gn_b, blocks["pre_tm"])

    o_f, o_r = gla_scan(z, _pad_rows(gla_gate_up, LANES), gla_gate_b, blocks["scan_tl"])
    y_b = gla_post(o_f, o_r, z, gla_norm_g, blocks["pre_tm"])
    return y_a, y_b


def _mixer_c(x, shift, scale, diff_w_in, diff_lambda, diff_subln_g, lambda_init, blocks):
    T = x.shape[1]
    perm = _qk_perm()
    w = jnp.concatenate([diff_w_in[:, :D_MODEL][:, perm],
                         diff_w_in[:, D_MODEL:2 * D_MODEL][:, perm],
                         diff_w_in[:, 2 * D_MODEL:]], axis=1).astype(BF16)
    cos_t, sin_t = _rope_tables(T)
    qkv = qkv_projection(x, shift, scale, w, cos_t, sin_t, blocks["proj_tm"], 1024)
    return diff_attention(qkv, diff_lambda, diff_subln_g, lambda_init, blocks["attn_tq"])


def _blocks(T):
    return {
        "proj_tm": min(1024, T),
        "pre_tm": min(256, T),
        "scan_tl": min(256, T),
        "attn_tq": min(256, T),
        "ln_tm": min(512, T),
        "ffn_tm": min(512, T),
    }


def kernel(x, c, ada_w, ada_b, ln_g, ln_b, ffn_w_in, ffn_w_out, ab_w_in, ab_w_out, rwkv_mu, rwkv_w0, rwkv_w_up, rwkv_a0, rwkv_a_up, rwkv_g_up, rwkv_k_k, rwkv_k_a, rwkv_r_k, rwkv_gn_g, rwkv_gn_b, gla_gate_up, gla_gate_b, gla_norm_g, diff_w_in, diff_w_out, diff_lambda, diff_subln_g):
    blocks = _blocks(x.shape[1])
    shift, scale, gate = ada_modulation(c, ada_w, ada_b)
    ffn_w_in_b = ffn_w_in.astype(BF16)
    ffn_w_out_b = ffn_w_out.astype(BF16)
    for i in range(DEPTH):
        j = i // 2
        m = 2 * i
        if i % 2 == 0:
            y_parts = _mixer_ab(
                x, shift[m], scale[m], ab_w_in[j], rwkv_mu[j], rwkv_w0[j], rwkv_w_up[j],
                rwkv_a0[j], rwkv_a_up[j], rwkv_g_up[j], rwkv_k_k[j], rwkv_k_a[j], rwkv_r_k[j],
                rwkv_gn_g[j], rwkv_gn_b[j], gla_gate_up[j], gla_gate_b[j], gla_norm_g[j], blocks)
            w_out = ab_w_out[j]
        else:
            y_parts = (_mixer_c(x, shift[m], scale[m], diff_w_in[j], diff_lambda[j],
                                diff_subln_g[j], _lambda_init(i), blocks),)
            w_out = diff_w_out[j]
        x = projection_layernorm(y_parts, w_out.astype(BF16), x, gate[m], ln_g[i, 0], ln_b[i, 0],
                                 blocks["ln_tm"])
        x = ffn_sublayer(x, shift[m + 1], scale[m + 1], gate[m + 1], ffn_w_in_b, ffn_w_out_b, i,
                         ln_g[i, 1], ln_b[i, 1], blocks["ffn_tm"], 512)
    return x
```

```python
import functools
import math

import jax
import jax.numpy as jnp
from jax import lax
from jax.experimental import pallas as pl
from jax.experimental.pallas import tpu as pltpu

F32 = jnp.float32
BF16 = jnp.bfloat16
HI = lax.Precision.HIGHEST

D_MODEL = 2048
DEPTH = 2
ALPHA = (2.0 * DEPTH) ** 0.25
LN_EPS = 1e-5

RWKV_HEAD_DIM = 64
RWKV_DIM = D_MODEL // 2
RWKV_HEADS = RWKV_DIM // RWKV_HEAD_DIM
DECAY_LORA = 96
ICL_LORA = 96
GATE_LORA = 256
W_DECAY_SCALE = 0.606531
RWKV_GN_EPS = 64e-5

GLA_HEADS = 4
GLA_VDIM = D_MODEL // 2
GLA_KDIM = GLA_VDIM // 2
GLA_DK = GLA_KDIM // GLA_HEADS
GLA_DV = GLA_VDIM // GLA_HEADS
GLA_GATE_LORA = 16
GLA_GATE_NORMALIZER = 16.0

DIFF_HEAD_DIM = 64
DIFF_V_DIM = 2 * DIFF_HEAD_DIM
DIFF_HEADS = D_MODEL // DIFF_V_DIM
ROPE_THETA = 10000.0

D_FF = -(-(8 * D_MODEL) // (3 * 256)) * 256

LANES = 128
VMEM_LIMIT = 56 * 1024 * 1024

CHUNK = 64
LN_SUB_ROWS = 128
GLA_SUB_ROWS = 256
LORA_PAD = 128
HALO_ROWS = 16
Z_SEGS = {
    "rkv": (0, 3 * RWKV_DIM, 0),
    "wd": (3 * RWKV_DIM, DECAY_LORA, 3072),
    "ad": (3 * RWKV_DIM + DECAY_LORA, ICL_LORA, 3200),
    "gd": (3 * RWKV_DIM + DECAY_LORA + ICL_LORA, GATE_LORA, 3328),
    "q": (3520, GLA_KDIM, 3584),
    "k": (3520 + GLA_KDIM, GLA_KDIM, 4096),
    "v": (3520 + 2 * GLA_KDIM, GLA_VDIM, 4608),
    "gg": (3520 + 2 * GLA_KDIM + GLA_VDIM, GLA_GATE_LORA, 6656),
    "og": (3520 + 2 * GLA_KDIM + GLA_VDIM + GLA_GATE_LORA, GLA_VDIM, 5632),
}
Z_A_COLS = 3584
Z_COLS = 7168


def _cparams(sem):
    return pltpu.CompilerParams(dimension_semantics=sem, vmem_limit_bytes=VMEM_LIMIT)


def _dot(a, b, prec=None):
    return jnp.dot(a, b, preferred_element_type=F32, precision=prec)


def _dot_nt(a, b, prec=None):
    return lax.dot_general(a, b, (((1,), (1,)), ((), ())), preferred_element_type=F32,
                           precision=prec)


def _dot_tn(a, b, prec=None):
    return lax.dot_general(a, b, (((0,), (0,)), ((), ())), preferred_element_type=F32,
                           precision=prec)


def _sigmoid(x):
    return 1.0 / (1.0 + jnp.exp(-x))


def _layer_norm_rows(u, g, b, eps):
    mu = jnp.mean(u, axis=-1, keepdims=True)
    d = u - mu
    var = jnp.mean(d * d, axis=-1, keepdims=True)
    return d * lax.rsqrt(var + eps) * g + b


def _ada_body(c_ref, w_ref, b_ref, o_ref):
    c = c_ref[...]
    sc = (c * _sigmoid(c)).astype(BF16)
    o_ref[0] = _dot(sc, w_ref[0].astype(BF16)) + b_ref[0]


def ada_modulation(c, ada_w, ada_b):
    B, D = c.shape
    n = ada_w.shape[0] * ada_w.shape[1]
    w = ada_w.reshape(n, D, 3 * D)
    b = ada_b.reshape(n, 1, 3 * D)
    rows = 8
    c_pad = jnp.pad(c, ((0, rows - B), (0, 0)))
    tn = 1024
    out = pl.pallas_call(
        _ada_body,
        grid=(n, 3 * D // tn),
        in_specs=[
            pl.BlockSpec((rows, D), lambda i, j: (0, 0)),
            pl.BlockSpec((1, D, tn), lambda i, j: (i, 0, j)),
            pl.BlockSpec((1, 1, tn), lambda i, j: (i, 0, j)),
        ],
        out_specs=pl.BlockSpec((1, rows, tn), lambda i, j: (i, 0, j)),
        out_shape=jax.ShapeDtypeStruct((n, rows, 3 * D), F32),
        compiler_params=_cparams(("parallel", "parallel")),
        name="ada_modulation",
    )(c_pad, w, b)
    mods = out[:, :B, :]
    shift, scale, gate = mods[..., :D], mods[..., D:2 * D], mods[..., 2 * D:]
    r3 = lambda t: t.reshape(n, B, 1, D)
    return r3(shift), r3(scale), r3(gate)


def _inproj_body(x_ref, sh_ref, sc_ref, w_ref, o_ref, h_ref):
    @pl.when(pl.program_id(2) == 0)
    def _():
        h_ref[...] = (x_ref[0] * (1.0 + sc_ref[0]) + sh_ref[0]).astype(BF16)

    o_ref[0] = _dot(h_ref[...], w_ref[...]).astype(o_ref.dtype)


def modulated_projection(x, shift, scale, w, out_dtype, tm, tn):
    B, T, D = x.shape
    N = w.shape[1]
    return pl.pallas_call(
        _inproj_body,
        grid=(B, T // tm, N // tn),
        in_specs=[
            pl.BlockSpec((1, tm, D), lambda b, i, j: (b, i, 0)),
            pl.BlockSpec((1, 1, D), lambda b, i, j: (b, 0, 0)),
            pl.BlockSpec((1, 1, D), lambda b, i, j: (b, 0, 0)),
            pl.BlockSpec((D, tn), lambda b, i, j: (0, j)),
        ],
        out_specs=pl.BlockSpec((1, tm, tn), lambda b, i, j: (b, i, j)),
        out_shape=jax.ShapeDtypeStruct((B, T, N), out_dtype),
        scratch_shapes=[pltpu.VMEM((tm, D), BF16)],
        compiler_params=_cparams(("parallel", "parallel", "arbitrary")),
        name="modulated_projection",
    )(x, shift, scale, w)


def _qkv_body(x_ref, sh_ref, sc_ref, w_ref, cos_ref, sin_ref, o_ref, h_ref, *, n_q, n_qk,
              q_scale):
    j = pl.program_id(2)

    @pl.when(j == 0)
    def _():
        h_ref[...] = (x_ref[0] * (1.0 + sc_ref[0]) + sh_ref[0]).astype(BF16)

    acc = _dot(h_ref[...], w_ref[...])
    tn = acc.shape[1]

    @pl.when(j < n_qk)
    def _():
        cos = cos_ref[...]
        sin = sin_ref[...]
        mult = jnp.where(j < n_q, q_scale, 1.0).astype(F32)
        for s in range(tn // LANES):
            xs = acc[:, s * LANES:(s + 1) * LANES]
            rot = pltpu.roll(xs, LANES // 2, axis=1)
            o_ref[0, :, s * LANES:(s + 1) * LANES] = ((xs * cos + rot * sin) * mult).astype(o_ref.dtype)

    @pl.when(j >= n_qk)
    def _():
        o_ref[0] = acc.astype(o_ref.dtype)


def qkv_projection(x, shift, scale, w, cos_t, sin_t, tm, tn):
    B, T, D = x.shape
    N = w.shape[1]
    n_q = D_MODEL // tn
    body = functools.partial(_qkv_body, n_q=n_q, n_qk=2 * n_q,
                             q_scale=DIFF_HEAD_DIM ** -0.5 * math.log2(math.e))
    return pl.pallas_call(
        body,
        grid=(B, T // tm, N // tn),
        in_specs=[
            pl.BlockSpec((1, tm, D), lambda b, i, j: (b, i, 0)),
            pl.BlockSpec((1, 1, D), lambda b, i, j: (b, 0, 0)),
            pl.BlockSpec((1, 1, D), lambda b, i, j: (b, 0, 0)),
            pl.BlockSpec((D, tn), lambda b, i, j: (0, j)),
            pl.BlockSpec((tm, LANES), lambda b, i, j: (i, 0)),
            pl.BlockSpec((tm, LANES), lambda b, i, j: (i, 0)),
        ],
        out_specs=pl.BlockSpec((1, tm, tn), lambda b, i, j: (b, i, j)),
        out_shape=jax.ShapeDtypeStruct((B, T, N), BF16),
        scratch_shapes=[pltpu.VMEM((tm, D), BF16)],
        compiler_params=_cparams(("parallel", "parallel", "arbitrary")),
        name="qkv_projection",
    )(x, shift, scale, w, cos_t, sin_t)


def _proj_ln_body(*refs, n_parts):
    y_refs = refs[:n_parts]
    w_ref, x_ref, gate_ref, g_ref, b_ref, o_ref = refs[n_parts:]
    tm = x_ref.shape[1]
    sub = min(LN_SUB_ROWS, tm)
    accs = []
    for s in range(tm // sub):
        rows = slice(s * sub, (s + 1) * sub)
        acc = None
        off = 0
        for yr in y_refs:
            k = yr.shape[-1]
            part = _dot(yr[0, rows, :], w_ref[off:off + k, :])
            acc = part if acc is None else acc + part
            off += k
        accs.append(acc)
    for s, acc in enumerate(accs):
        rows = slice(s * sub, (s + 1) * sub)
        u = ALPHA * x_ref[0, rows, :] + (1.0 + gate_ref[0]) * acc
        o_ref[0, rows, :] = _layer_norm_rows(u, g_ref[...], b_ref[...], LN_EPS)


def projection_layernorm(y_parts, w, x, gate, ln_g, ln_b, tm):
    B, T, D = x.shape
    K = w.shape[0]
    in_specs = [pl.BlockSpec((1, tm, yp.shape[-1]), lambda b, i: (b, i, 0)) for yp in y_parts]
    in_specs += [
        pl.BlockSpec((K, D), lambda b, i: (0, 0)),
        pl.BlockSpec((1, tm, D), lambda b, i: (b, i, 0)),
        pl.BlockSpec((1, 1, D), lambda b, i: (b, 0, 0)),
        pl.BlockSpec((1, D), lambda b, i: (0, 0)),
        pl.BlockSpec((1, D), lambda b, i: (0, 0)),
    ]
    return pl.pallas_call(
        functools.partial(_proj_ln_body, n_parts=len(y_parts)),
        grid=(B, T // tm),
        in_specs=in_specs,
        out_specs=pl.BlockSpec((1, tm, D), lambda b, i: (b, i, 0)),
        out_shape=jax.ShapeDtypeStruct((B, T, D), F32),
        compiler_params=_cparams(("parallel", "parallel")),
        name="projection_layernorm",
    )(*y_parts, w, x, gate, ln_g.reshape(1, D), ln_b.reshape(1, D))


def _ffn_body(x_ref, sh_ref, sc_ref, gate_ref, wg_ref, wu_ref, wo_ref, g_ref, b_ref, o_ref,
              h_ref, acc_ref):
    j = pl.program_id(2)

    @pl.when(j == 0)
    def _():
        h_ref[...] = (x_ref[0] * (1.0 + sc_ref[0]) + sh_ref[0]).astype(BF16)
        acc_ref[...] = jnp.zeros_like(acc_ref)

    h = h_ref[...]
    gt = _dot(h, wg_ref[0])
    up = _dot(h, wu_ref[0])
    act = (gt * _sigmoid(gt) * up).astype(BF16)
    acc_ref[...] += _dot(act, wo_ref[0])

    @pl.when(j == pl.num_programs(2) - 1)
    def _():
        u = ALPHA * x_ref[0] + (1.0 + gate_ref[0]) * acc_ref[...]
        o_ref[0] = _layer_norm_rows(u, g_ref[...], b_ref[...], LN_EPS)


def ffn_sublayer(x, shift, scale, gate, w_in, w_out, layer, ln_g, ln_b, tm, tf):
    B, T, D = x.shape
    F = w_out.shape[1]
    nf = F // tf
    return pl.pallas_call(
        _ffn_body,
        grid=(B, T // tm, nf),
        in_specs=[
            pl.BlockSpec((1, tm, D), lambda b, i, j: (b, i, 0)),
            pl.BlockSpec((1, 1, D), lambda b, i, j: (b, 0, 0)),
            pl.BlockSpec((1, 1, D), lambda b, i, j: (b, 0, 0)),
            pl.BlockSpec((1, 1, D), lambda b, i, j: (b, 0, 0)),
            pl.BlockSpec((1, D, tf), lambda b, i, j: (layer, 0, j)),
            pl.BlockSpec((1, D, tf), lambda b, i, j: (layer, 0, j + nf)),
            pl.BlockSpec((1, tf, D), lambda b, i, j: (layer, j, 0)),
            pl.BlockSpec((1, D), lambda b, i, j: (0, 0)),
            pl.BlockSpec((1, D), lambda b, i, j: (0, 0)),
        ],
        out_specs=pl.BlockSpec((1, tm, D), lambda b, i, j: (b, i, 0)),
        out_shape=jax.ShapeDtypeStruct((B, T, D), F32),
        scratch_shapes=[pltpu.VMEM((tm, D), BF16), pltpu.VMEM((tm, D), F32)],
        compiler_params=_cparams(("parallel", "parallel", "arbitrary")),
        name="ffn_sublayer",
    )(x, shift, scale, gate, w_in, w_in, w_out, ln_g.reshape(1, D), ln_b.reshape(1, D))


ONES_ROWS = 16


KEY_CHUNK = 512


def _attn_body(q0_ref, qn_ref, k_ref, v_ref, lam_ref, g_ref, o_ref, vt_ref, s_ref, m_ref,
               *, lambda_init):
    dv = DIFF_V_DIM
    T = k_ref.shape[1]
    lane = lax.broadcasted_iota(jnp.int32, (1, LANES), 1)
    comp0 = (lane // (DIFF_HEAD_DIM // 2)) % 2 == 0

    def comps(q):
        zero = jnp.zeros_like(q)
        return (jnp.where(comp0, q, zero), jnp.where(comp0, zero, q))

    def chunk_scores(qcs, rows):
        kc = k_ref[0, rows, :]
        return [_dot_nt(kc, qc) for qc in qcs]

    def col_max(s):
        return jnp.max(s, axis=0, keepdims=True)

    chunks = [slice(c * KEY_CHUNK, (c + 1) * KEY_CHUNK) for c in range(T // KEY_CHUNK)]

    @pl.when(pl.program_id(2) == 0)
    def _():
        vt_ref[0:dv, :] = v_ref[0].astype(F32).T.astype(BF16)
        vt_ref[dv:, :] = jnp.ones((ONES_ROWS, T), BF16)
        qcs = comps(q0_ref[0])
        m = [None, None]
        for rows in chunks:
            s = chunk_scores(qcs, rows)
            for j in range(2):
                s_ref[j, rows, :] = s[j]
                m[j] = col_max(s[j]) if m[j] is None else jnp.maximum(m[j], col_max(s[j]))
        for j in range(2):
            m_ref[j] = m[j]

    qcs = comps(qn_ref[0])
    m_cur = [m_ref[0], m_ref[1]]
    m_new = [None, None]
    acc = [None, None]
    for rows in chunks:
        s_new = chunk_scores(qcs, rows)
        for j in range(2):
            e = jnp.exp2(s_ref[j, rows, :] - m_cur[j]).astype(BF16)
            part = _dot(vt_ref[:, rows], e)
            acc[j] = part if acc[j] is None else acc[j] + part
        for j in range(2):
            s_ref[j, rows, :] = s_new[j]
            m_new[j] = col_max(s_new[j]) if m_new[j] is None else jnp.maximum(m_new[j], col_max(s_new[j]))
    for j in range(2):
        m_ref[j] = m_new[j]

    lp = lam_ref[...]
    lam = (jnp.exp(jnp.sum(lp[0:1] * lp[1:2], axis=-1, keepdims=True))
           - jnp.exp(jnp.sum(lp[2:3] * lp[3:4], axis=-1, keepdims=True)) + lambda_init)
    outs = [a[0:dv] / a[dv:dv + 1] for a in acc]
    o = (outs[0] - lam * outs[1]).T
    ms = jnp.mean(o * o, axis=-1, keepdims=True)
    o = o * lax.rsqrt(ms + LN_EPS) * g_ref[...] * (1.0 - lambda_init)
    o_ref[0] = o.astype(o_ref.dtype)


def diff_attention(qkv, lam_params, subln_g, lambda_init, tq):
    B, T, _ = qkv.shape
    H = DIFF_HEADS
    nq = T // tq
    return pl.pallas_call(
        functools.partial(_attn_body, lambda_init=lambda_init),
        grid=(B, H, nq),
        in_specs=[
            pl.BlockSpec((1, tq, LANES), lambda b, h, i: (b, 0, h)),
            pl.BlockSpec((1, tq, LANES), lambda b, h, i: (b, jnp.minimum(i + 1, nq - 1), h)),
            pl.BlockSpec((1, T, LANES), lambda b, h, i: (b, 0, H + h)),
            pl.BlockSpec((1, T, LANES), lambda b, h, i: (b, 0, 2 * H + h)),
            pl.BlockSpec((4, DIFF_HEAD_DIM), lambda b, h, i: (0, 0)),
            pl.BlockSpec((1, LANES), lambda b, h, i: (0, 0)),
        ],
        out_specs=pl.BlockSpec((1, tq, LANES), lambda b, h, i: (b, i, h)),
        out_shape=jax.ShapeDtypeStruct((B, T, H * DIFF_V_DIM), BF16),
        scratch_shapes=[pltpu.VMEM((DIFF_V_DIM + ONES_ROWS, T), BF16),
                        pltpu.VMEM((2, T, tq), F32),
                        pltpu.VMEM((2, 1, tq), F32)],
        compiler_params=_cparams(("parallel", "parallel", "arbitrary")),
        name="diff_attention",
    )(qkv, qkv, qkv, qkv, lam_params, subln_g.reshape(1, LANES))


def _per_head_sum(x, head_dim):
    lane = lax.broadcasted_iota(jnp.int32, (1, LANES), 1)
    lo = lane < head_dim
    s_lo = jnp.sum(jnp.where(lo, x, 0.0), axis=-1, keepdims=True)
    s_hi = jnp.sum(jnp.where(lo, 0.0, x), axis=-1, keepdims=True)
    return jnp.where(lo, s_lo, s_hi)


def _rwkv_pre_body(z_ref, zp_ref, zn_ref, mu_ref, w0_ref, wup_ref, a0_ref, aup_ref, gup_ref,
                   kk_ref, ka_ref, r_o, k_o, v_o, kk_o, ba_o, lw_o, g_o, *, tm):
    i = pl.program_id(1)
    last = pl.num_programs(1) - 1
    row = lax.broadcasted_iota(jnp.int32, (tm, 1), 0)
    C = RWKV_DIM

    def shifted(lo, hi):
        z = z_ref[0, :, lo:hi].astype(F32)
        prev = jnp.where(i == 0, 0.0, zp_ref[0, HALO_ROWS - 1:HALO_ROWS, lo:hi].astype(F32))
        nxt = jnp.where(i == last, 0.0, zn_ref[0, 0:1, lo:hi].astype(F32))
        z_dn = jnp.where(row == 0, prev, pltpu.roll(z, 1, axis=0))
        z_up = jnp.where(row == tm - 1, nxt, pltpu.roll(z, tm - 1, axis=0))
        return z + (0.5 * (z_dn + z_up) - z) * mu_ref[:, lo:hi]

    r = shifted(0, C)
    k = shifted(C, 2 * C)
    v = shifted(2 * C, 3 * C)
    wd = shifted(3 * C, 3 * C + LORA_PAD)
    ad = shifted(3 * C + LORA_PAD, 3 * C + 2 * LORA_PAD)
    gd = shifted(3 * C + 2 * LORA_PAD, 3 * C + 2 * LORA_PAD + GATE_LORA)

    r_o[0] = r.astype(r_o.dtype)
    v_o[0] = v.astype(v_o.dtype)
    twd = jnp.tanh(wd).astype(BF16)
    for d in range(2):
        lw_o[d, 0] = -W_DECAY_SCALE * _sigmoid(w0_ref[d] + _dot(twd, wup_ref[d].astype(BF16)))
    a = _sigmoid(a0_ref[...] + _dot(ad.astype(BF16), aup_ref[...].astype(BF16)))
    g_o[0] = _dot(_sigmoid(gd).astype(BF16), gup_ref[...].astype(BF16)).astype(g_o.dtype)
    k_o[0] = (k * (1.0 + (a - 1.0) * ka_ref[...])).astype(k_o.dtype)
    kk0 = k * kk_ref[...]
    for s in range(C // LANES):
        sl = slice(s * LANES, (s + 1) * LANES)
        x = kk0[:, sl]
        nrm = jnp.maximum(jnp.sqrt(_per_head_sum(x * x, RWKV_HEAD_DIM)), 1e-12)
        kk = x / nrm
        kk_o[0, :, sl] = kk.astype(kk_o.dtype)
        ba_o[0, :, sl] = (kk * a[:, sl]).astype(ba_o.dtype)


def rwkv_pre(z, mu_p, w0, wup_p, a0, aup_p, gup, k_k, k_a, tm):
    B, T, _ = z.shape
    C = RWKV_DIM
    nbh = tm // HALO_ROWS
    nh = T // HALO_ROWS
    bt = jax.ShapeDtypeStruct((B, T, C), BF16)
    vec = lambda t: t.reshape(1, C)
    full = lambda shp: pl.BlockSpec(shp, lambda b, i: (0,) * len(shp))
    body = functools.partial(_rwkv_pre_body, tm=tm)
    return pl.pallas_call(
        body,
        grid=(B, T // tm),
        in_specs=[
            pl.BlockSpec((1, tm, Z_A_COLS), lambda b, i: (b, i, 0)),
            pl.BlockSpec((1, HALO_ROWS, Z_A_COLS), lambda b, i: (b, jnp.maximum(i * nbh - 1, 0), 0)),
            pl.BlockSpec((1, HALO_ROWS, Z_A_COLS),
                         lambda b, i: (b, jnp.minimum((i + 1) * nbh, nh - 1), 0)),
            full((1, Z_A_COLS)),
            full((2, 1, C)),
            full((2, LORA_PAD, C)),
            full((1, C)),
            full((LORA_PAD, C)),
            full((GATE_LORA, C)),
            full((1, C)),
            full((1, C)),
        ],
        out_specs=[pl.BlockSpec((1, tm, C), lambda b, i: (b, i, 0))] * 5
        + [pl.BlockSpec((2, 1, tm, C), lambda b, i: (0, b, i, 0)),
           pl.BlockSpec((1, tm, C), lambda b, i: (b, i, 0))],
        out_shape=[bt] * 5 + [jax.ShapeDtypeStruct((2, B, T, C), F32), bt],
        compiler_params=_cparams(("parallel", "parallel")),
        name="rwkv_pre",
    )(z, z, z, mu_p, w0.reshape(2, 1, C), wup_p, vec(a0), aup_p, gup, vec(k_k), vec(k_a))


def _bf(x):
    return x.astype(BF16)


def _segmented_cumsum(x, seg):
    pos = lax.broadcasted_iota(jnp.int32, (x.shape[0], 1), 0) % seg
    s = 1
    while s < seg:
        x = x + jnp.where(pos >= s, pltpu.roll(x, s, axis=0), 0.0)
        s *= 2
    return x


def _segment_totals(x, seg):
    parts = []
    for c in range(x.shape[0] // seg):
        t = jnp.sum(x[c * seg:(c + 1) * seg], axis=0, keepdims=True)
        parts.append(jnp.broadcast_to(t, (seg, x.shape[1])))
    return jnp.concatenate(parts, axis=0)


def _decay_factors(lw, reverse):
    tot = _segment_totals(lw, CHUNK)
    cum = _segmented_cumsum(lw, CHUNK)
    if reverse:
        cum = tot - cum + lw
    return cum, tot


def _rwkv_chunks_local(chunks):
    L = CHUNK
    lane = lax.broadcasted_iota(jnp.int32, (1, LANES), 1)
    h0 = lane < RWKV_HEAD_DIM

    def stack(x):
        return jnp.concatenate([jnp.where(h0, x, 0.0), jnp.where(h0, 0.0, x)], axis=0)

    pre = []
    for r, k, v, kk, ba, lw, cum, tot, strict, incl in chunks:
        g_ex = jnp.exp(cum - lw)
        g_inv = jnp.exp(-cum)
        g_rem = jnp.exp(tot - cum)
        a_st = stack(-kk * g_ex)
        r_st = stack(r * jnp.exp(cum))
        lhs = _bf(jnp.concatenate([a_st, r_st], axis=0))
        rhs = _bf(jnp.concatenate([stack(ba * g_inv), stack(k * g_inv)], axis=0))
        t_b = _bf(jnp.concatenate([stack(ba * g_rem), stack(k * g_rem)], axis=0))
        pre.append((a_st, r_st, lhs, rhs, t_b, _bf(stack(v))))

    ps = [_dot_nt(lhs, rhs) for _, _, lhs, rhs, _, _ in pre]
    m_ab, m_ak, m_r = [], [], []
    for p, ch in zip(ps, chunks):
        strict, incl = ch[8], ch[9]
        m_ab.append(jnp.where(strict, p[:2 * L, :2 * L], 0.0))
        m_ak.append(_bf(jnp.where(strict, p[:2 * L, 2 * L:], 0.0)))
        m_r.append(_bf(jnp.concatenate([jnp.where(incl, p[2 * L:, :2 * L], 0.0),
                                        jnp.where(incl, p[2 * L:, 2 * L:], 0.0)], axis=1)))

    akv = [_dot(m, pr[5]) for m, pr in zip(m_ak, pre)]
    xs = [jnp.concatenate([pr[0], t], axis=1) for pr, t in zip(pre, akv)]
    ns = m_ab
    steps = int(math.log2(L))
    for it in range(steps):
        n_bs = [_bf(n) for n in ns]
        xs = [x + _dot(n_b, _bf(x)) for x, n_b in zip(xs, n_bs)]
        if it + 1 < steps:
            ns = [_dot(n_b, n_b) for n_b in n_bs]

    out = []
    wu = [(_bf(x[:, :LANES]), _bf(x[:, LANES:])) for x in xs]
    qys = [_dot(m, jnp.concatenate([jnp.concatenate([w_b, ul_b], axis=1),
                                    jnp.concatenate([jnp.zeros_like(pr[5]), pr[5]], axis=1)], axis=0))
           for m, (w_b, ul_b), pr in zip(m_r, wu, pre)]
    gps = [_dot_tn(w_b, pr[4][:2 * L]) for (w_b, _), pr in zip(wu, pre)]
    hs = [_dot_tn(jnp.concatenate([ul_b, pr[5]], axis=0), pr[4])
          for (_, ul_b), pr in zip(wu, pre)]
    for qy, gp, h, pr in zip(qys, gps, hs, pre):
        out.append((_bf(pr[1] + qy[:, :LANES]), qy[:, LANES:], _bf(gp), h))
    return out


def _rwkv_scan_body(rf, kf, vf, kkf, baf, lwf, rb, kb, vb, kkb, bab, lwb, yf_ref, yb_ref, s_ref,
                    *, n_chunks):
    L = CHUNK

    @pl.when(pl.program_id(2) == 0)
    def _():
        s_ref[...] = jnp.zeros_like(s_ref)

    r2 = lax.broadcasted_iota(jnp.int32, (2 * L, 2 * L), 0)
    c2 = lax.broadcasted_iota(jnp.int32, (2 * L, 2 * L), 1)
    same_head = (r2 // L) == (c2 // L)
    dt = r2 % L - c2 % L

    chunks, g_tots = [], []
    for d, refs in ((0, (rf, kf, vf, kkf, baf, lwf)), (1, (rb, kb, vb, kkb, bab, lwb))):
        r_ref, k_ref, v_ref, kk_ref, ba_ref, lw_ref = refs
        strict = jnp.where(same_head, dt if d == 0 else -dt, -1) > 0
        incl = strict | (r2 == c2)
        lw_all = lw_ref[0, 0]
        cum_all, tot_all = _decay_factors(lw_all, reverse=(d == 1))
        g_tot_all = jnp.exp(tot_all)
        for c in range(n_chunks):
            rows = slice(c * L, (c + 1) * L)
            chunks.append((r_ref[0, rows, :], k_ref[0, rows, :], v_ref[0, rows, :],
                           kk_ref[0, rows, :], ba_ref[0, rows, :], lw_all[rows], cum_all[rows],
                           tot_all[rows], strict, incl))
            g_tots.append(g_tot_all[c * L:c * L + 1])
    local = _rwkv_chunks_local(chunks)

    s2 = [s_ref[0], s_ref[1]]
    for j in range(n_chunks):
        for d, y_ref in ((0, yf_ref), (1, yb_ref)):
            c = j if d == 0 else n_chunks - 1 - j
            q_b, yl_st, gp_b, h = local[d * n_chunks + c]
            s_b = _bf(s2[d])
            y_st = _dot_nt(q_b, s_b) + yl_st
            y_ref[0, c * L:(c + 1) * L, :] = y_st[:L] + y_st[L:]
            s2[d] = s2[d] * g_tots[d * n_chunks + c] + _dot(s_b, gp_b) + h
    s_ref[0] = s2[0]
    s_ref[1] = s2[1]


def rwkv_scan(r, k, v, kk, ba, lw, tl):
    B, T, C = r.shape
    n_pairs = C // LANES
    nblk = T // tl
    fwd = pl.BlockSpec((1, tl, LANES), lambda b, p, i: (b, i, p))
    bwd = pl.BlockSpec((1, tl, LANES), lambda b, p, i: (b, nblk - 1 - i, p))
    lw_f = pl.BlockSpec((1, 1, tl, LANES), lambda b, p, i: (0, b, i, p))
    lw_b = pl.BlockSpec((1, 1, tl, LANES), lambda b, p, i: (1, b, nblk - 1 - i, p))
    y_sds = jax.ShapeDtypeStruct((B, T, C), F32)
    return pl.pallas_call(
        functools.partial(_rwkv_scan_body, n_chunks=tl // CHUNK),
        grid=(B, n_pairs, nblk),
        in_specs=[fwd] * 5 + [lw_f] + [bwd] * 5 + [lw_b],
        out_specs=[fwd, bwd],
        out_shape=[y_sds, y_sds],
        scratch_shapes=[pltpu.VMEM((2, LANES, LANES), F32)],
        compiler_params=_cparams(("parallel", "parallel", "arbitrary")),
        name="rwkv_scan",
    )(r, k, v, kk, ba, lw, r, k, v, kk, ba, lw)


def _rwkv_post_body(yf_ref, yb_ref, r_ref, k_ref, v_ref, g_ref, rk_ref, gg_ref, gb_ref, o_ref):
    C = RWKV_DIM
    for s in range(C // LANES):
        sl = slice(s * LANES, (s + 1) * LANES)
        y = yf_ref[0, :, sl] + yb_ref[0, :, sl]
        inv_n = 1.0 / RWKV_HEAD_DIM
        mu = _per_head_sum(y, RWKV_HEAD_DIM) * inv_n
        dlt = y - mu
        var = _per_head_sum(dlt * dlt, RWKV_HEAD_DIM) * inv_n
        yn = dlt * lax.rsqrt(var + RWKV_GN_EPS) * gg_ref[:, sl] + gb_ref[:, sl]
        rk = r_ref[0, :, sl].astype(F32) * k_ref[0, :, sl].astype(F32)
        bonus = _per_head_sum(rk * rk_ref[:, sl], RWKV_HEAD_DIM)
        out = (yn + bonus * v_ref[0, :, sl].astype(F32)) * g_ref[0, :, sl].astype(F32)
        o_ref[0, :, sl] = out.astype(o_ref.dtype)


def rwkv_post(yf, yb, r, k, v, g, r_k, gn_g, gn_b, tm):
    B, T, C = r.shape
    spec = pl.BlockSpec((1, tm, C), lambda b, i: (b, i, 0))
    vspec = pl.BlockSpec((1, C), lambda b, i: (0, 0))
    return pl.pallas_call(
        _rwkv_post_body,
        grid=(B, T // tm),
        in_specs=[spec] * 6 + [vspec] * 3,
        out_specs=spec,
        out_shape=jax.ShapeDtypeStruct((B, T, C), BF16),
        compiler_params=_cparams(("parallel", "parallel")),
        name="rwkv_post",
    )(yf, yb, r, k, v, g, r_k.reshape(1, C), gn_g.reshape(1, C), gn_b.reshape(1, C))


def _log_sigmoid(x):
    return jnp.minimum(x, 0.0) - jnp.log(1.0 + jnp.exp(-jnp.abs(x)))


def _gla_body(qf, kf, vf, ggf, qb_ref, kb_ref, vb_ref, ggb, gup_ref, gb_ref, of_ref, ob_ref, st_ref,
              *, n_chunks):
    L = CHUNK
    tl = n_chunks * L

    @pl.when(pl.program_id(2) == 0)
    def _():
        st_ref[...] = jnp.zeros_like(st_ref)

    sub = min(GLA_SUB_ROWS, tl)
    ri = lax.broadcasted_iota(jnp.int32, (sub, sub), 0)
    ci = lax.broadcasted_iota(jnp.int32, (sub, sub), 1)
    same_chunk = (ri // L) == (ci // L)
    chunk_rows = [slice(c * L, (c + 1) * L) for c in range(n_chunks)]

    loc = []
    for d, (q_ref, k_ref, v_ref, gg_ref) in enumerate(((qf, kf, vf, ggf), (qb_ref, kb_ref, vb_ref, ggb))):
        lg = _log_sigmoid(_dot(_bf(gg_ref[0]), _bf(gup_ref[d])) + gb_ref[d]) * (1.0 / GLA_GATE_NORMALIZER)
        cum, tot = _decay_factors(lg, reverse=(d == 1))
        k = k_ref[0].astype(F32)
        loc.append(dict(
            v=_bf(v_ref[0]),
            qb=_bf(q_ref[0].astype(F32) * (GLA_DK ** -0.5) * jnp.exp(cum)),
            kb=_bf(k * jnp.exp(-cum)),
            kt=_bf(k * jnp.exp(tot - cum)),
            g_tot=jnp.exp(tot),
            incl=jnp.where(same_chunk, (ri - ci) if d == 0 else (ci - ri), -1) >= 0))
    blocks = [slice(s * sub, (s + 1) * sub) for s in range(tl // sub)]
    atts = [[jnp.where(lc["incl"], _dot_nt(lc["qb"][b], lc["kb"][b]), 0.0) for b in blocks] for lc in loc]
    o_intra = [jnp.concatenate([_dot(_bf(a), lc["v"][b]) for a, b in zip(att, blocks)], axis=0)
               for att, lc in zip(atts, loc)]
    kvs = [[_dot_tn(lc["v"][r], lc["kt"][r]) for r in chunk_rows] for lc in loc]

    st = [st_ref[0], st_ref[1]]
    for j in range(n_chunks):
        for d, o_ref in ((0, of_ref), (1, ob_ref)):
            c = j if d == 0 else n_chunks - 1 - j
            rows = chunk_rows[c]
            o_ref[0, rows, :] = o_intra[d][rows] + _dot_nt(loc[d]["qb"][rows], _bf(st[d]))
            st[d] = st[d] * loc[d]["g_tot"][c * L:c * L + 1] + kvs[d][c]
    st_ref[0] = st[0]
    st_ref[1] = st[1]


def gla_scan(z, gate_up_p, gate_b, tl):
    B, T, _ = z.shape
    H = GLA_HEADS
    nblk = T // tl
    qc = Z_SEGS["q"][2] // GLA_DK
    kc = Z_SEGS["k"][2] // GLA_DK
    vc = Z_SEGS["v"][2] // GLA_DV
    gc = Z_SEGS["gg"][2] // LANES
    specs = []
    for tmap in (lambda i: i, lambda i: nblk - 1 - i):
        specs += [
            pl.BlockSpec((1, tl, GLA_DK), lambda b, h, i, tmap=tmap: (b, tmap(i), qc + h)),
            pl.BlockSpec((1, tl, GLA_DK), lambda b, h, i, tmap=tmap: (b, tmap(i), kc + h)),
            pl.BlockSpec((1, tl, GLA_DV), lambda b, h, i, tmap=tmap: (b, tmap(i), vc + h)),
            pl.BlockSpec((1, tl, LANES), lambda b, h, i, tmap=tmap: (b, tmap(i), gc)),
        ]
    o_sds = jax.ShapeDtypeStruct((B, T, GLA_VDIM), F32)
    return pl.pallas_call(
        functools.partial(_gla_body, n_chunks=tl // CHUNK),
        grid=(B, H, nblk),
        in_specs=specs + [
            pl.BlockSpec((2, LANES, GLA_DK), lambda b, h, i: (0, 0, h)),
            pl.BlockSpec((2, 1, GLA_DK), lambda b, h, i: (0, 0, h)),
        ],
        out_specs=[pl.BlockSpec((1, tl, GLA_DV), lambda b, h, i: (b, i, h)),
                   pl.BlockSpec((1, tl, GLA_DV), lambda b, h, i: (b, nblk - 1 - i, h))],
        out_shape=[o_sds, o_sds],
        scratch_shapes=[pltpu.VMEM((2, GLA_DV, GLA_DK), F32)],
        compiler_params=_cparams(("parallel", "parallel", "arbitrary")),
        name="gla_scan",
    )(z, z, z, z, z, z, z, z, gate_up_p, gate_b.reshape(2, 1, GLA_KDIM))


def _gla_post_body(of_ref, ob_ref, og_ref, g_ref, y_ref):
    o = of_ref[0] + ob_ref[0]
    ms = jnp.mean(o * o, axis=-1, keepdims=True)
    on = o * lax.rsqrt(ms + LN_EPS) * g_ref[...]
    og = og_ref[0].astype(F32)
    y_ref[0] = (on * (og * _sigmoid(og))).astype(y_ref.dtype)


def gla_post(o_f, o_b, z, norm_g, tm):
    B, T, _ = o_f.shape
    H = GLA_HEADS
    oc = Z_SEGS["og"][2] // GLA_DV
    return pl.pallas_call(
        _gla_post_body,
        grid=(B, T // tm, H),
        in_specs=[
            pl.BlockSpec((1, tm, GLA_DV), lambda b, i, h: (b, i, h)),
            pl.BlockSpec((1, tm, GLA_DV), lambda b, i, h: (b, i, h)),
            pl.BlockSpec((1, tm, GLA_DV), lambda b, i, h: (b, i, oc + h)),
            pl.BlockSpec((1, GLA_DV), lambda b, i, h: (0, 0)),
        ],
        out_specs=pl.BlockSpec((1, tm, GLA_DV), lambda b, i, h: (b, i, h)),
        out_shape=jax.ShapeDtypeStruct((B, T, GLA_VDIM), BF16),
        compiler_params=_cparams(("parallel", "parallel", "parallel")),
        name="gla_post",
    )(o_f, o_b, z, norm_g.reshape(1, GLA_DV))


def _pad_ab_columns(w):
    out = jnp.zeros((w.shape[0], Z_COLS), w.dtype)
    for o_start, width, n_start in Z_SEGS.values():
        out = lax.dynamic_update_slice(out, w[:, o_start:o_start + width], (0, n_start))
    return out


def _pad_rows(w, rows):
    pad = [(0, 0)] * w.ndim
    pad[-2] = (0, rows - w.shape[-2])
    return jnp.pad(w, pad)


def _permute_qk_columns(w):
    half = DIFF_HEAD_DIM // 2
    w5 = w.reshape(w.shape[0], DIFF_HEADS, 2, 2, half)
    return w5.transpose(0, 1, 3, 2, 4).reshape(w.shape)


def _rope_tables(T):
    half = DIFF_HEAD_DIM // 2
    inv = ROPE_THETA ** (-jnp.arange(0, DIFF_HEAD_DIM, 2, dtype=F32) / DIFF_HEAD_DIM)
    ang = jnp.arange(T, dtype=F32)[:, None] * inv[None, :]
    cos, sin = jnp.cos(ang), jnp.sin(ang)
    cos_t = jnp.tile(cos, (1, 4))
    sin_t = jnp.concatenate([-sin, -sin, sin, sin], axis=1)
    return cos_t, sin_t


def _lambda_init(layer):
    return 0.8 - 0.6 * math.exp(-0.3 * layer)


def _mixer_ab(x, shift, scale, ab_w_in, rwkv_mu, rwkv_w0, rwkv_w_up, rwkv_a0, rwkv_a_up,
              rwkv_g_up, rwkv_k_k, rwkv_k_a, rwkv_r_k, rwkv_gn_g, rwkv_gn_b,
              gla_gate_up, gla_gate_b, gla_norm_g, blocks):
    w_p = _pad_ab_columns(ab_w_in.astype(BF16))
    z = modulated_projection(x, shift, scale, w_p, BF16, blocks["proj_tm"], 1024)

    mu_p = jnp.zeros((1, Z_A_COLS), F32)
    for name in ("rkv", "wd", "ad", "gd"):
        o_start, width, n_start = Z_SEGS[name]
        mu_p = lax.dynamic_update_slice(mu_p, rwkv_mu[None, o_start:o_start + width], (0, n_start))
    r, k, v, kk, ba, lw, g = rwkv_pre(
        z, mu_p, rwkv_w0, _pad_rows(rwkv_w_up, LORA_PAD), rwkv_a0, _pad_rows(rwkv_a_up, LORA_PAD),
        rwkv_g_up, rwkv_k_k, rwkv_k_a, blocks["pre_tm"])
    y_f, y_r = rwkv_scan(r, k, v, kk, ba, lw, blocks["rwkv_tl"])
    y_a = rwkv_post(y_f, y_r, r, k, v, g, rwkv_r_k, rwkv_gn_g, rwkv_gn_b, blocks["pre_tm"])

    o_f, o_r = gla_scan(z, _pad_rows(gla_gate_up, LANES), gla_gate_b, blocks["scan_tl"])
    y_b = gla_post(o_f, o_r, z, gla_norm_g, blocks["post_tm"])
    return y_a, y_b


def _mixer_c(x, shift, scale, diff_w_in, diff_lambda, diff_subln_g, lambda_init, blocks):
    T = x.shape[1]
    w_b = diff_w_in.astype(BF16)
    w = jnp.concatenate([_permute_qk_columns(w_b[:, :D_MODEL]),
                         _permute_qk_columns(w_b[:, D_MODEL:2 * D_MODEL]),
                         w_b[:, 2 * D_MODEL:]], axis=1)
    cos_t, sin_t = _rope_tables(T)
    qkv = qkv_projection(x, shift, scale, w, cos_t, sin_t, blocks["proj_tm"], 1024)
    return diff_attention(qkv, diff_lambda, diff_subln_g, lambda_init, blocks["attn_tq"])


def _blocks(T):
    return {
        "proj_tm": min(1024, T),
        "pre_tm": min(256, T),
        "post_tm": min(1024, T),
        "rwkv_tl": min(512, T),
        "scan_tl": min(512, T),
        "attn_tq": min(256, T),
        "ln_tm": min(512, T),
        "ffn_tm": min(512, T),
    }


def kernel(x, c, ada_w, ada_b, ln_g, ln_b, ffn_w_in, ffn_w_out, ab_w_in, ab_w_out, rwkv_mu, rwkv_w0, rwkv_w_up, rwkv_a0, rwkv_a_up, rwkv_g_up, rwkv_k_k, rwkv_k_a, rwkv_r_k, rwkv_gn_g, rwkv_gn_b, gla_gate_up, gla_gate_b, gla_norm_g, diff_w_in, diff_w_out, diff_lambda, diff_subln_g):
    blocks = _blocks(x.shape[1])
    shift, scale, gate = ada_modulation(c, ada_w, ada_b)
    ffn_w_in_b = ffn_w_in.astype(BF16)
    ffn_w_out_b = ffn_w_out.astype(BF16)
    for i in range(DEPTH):
        j = i // 2
        m = 2 * i
        if i % 2 == 0:
            y_parts = _mixer_ab(
                x, shift[m], scale[m], ab_w_in[j], rwkv_mu[j], rwkv_w0[j], rwkv_w_up[j],
                rwkv_a0[j], rwkv_a_up[j], rwkv_g_up[j], rwkv_k_k[j], rwkv_k_a[j], rwkv_r_k[j],
                rwkv_gn_g[j], rwkv_gn_b[j], gla_gate_up[j], gla_gate_b[j], gla_norm_g[j], blocks)
            w_out = ab_w_out[j]
        else:
            y_parts = (_mixer_c(x, shift[m], scale[m], diff_w_in[j], diff_lambda[j],
                                diff_subln_g[j], _lambda_init(i), blocks),)
            w_out = diff_w_out[j]
        x = projection_layernorm(y_parts, w_out.astype(BF16), x, gate[m], ln_g[i, 0], ln_b[i, 0],
                                 blocks["ln_tm"])
        x = ffn_sublayer(x, shift[m + 1], scale[m + 1], gate[m + 1], ffn_w_in_b, ffn_w_out_b, i,
                         ln_g[i, 1], ln_b[i, 1], blocks["ffn_tm"], 512)
    return x
```

```python
import functools
import math

import jax
import jax.numpy as jnp
from jax import lax
from jax.experimental import pallas as pl
from jax.experimental.pallas import tpu as pltpu

F32 = jnp.float32
BF16 = jnp.bfloat16
HI = lax.Precision.HIGHEST

D_MODEL = 2048
DEPTH = 2
ALPHA = (2.0 * DEPTH) ** 0.25
LN_EPS = 1e-5

RWKV_HEAD_DIM = 64
RWKV_DIM = D_MODEL // 2
RWKV_HEADS = RWKV_DIM // RWKV_HEAD_DIM
DECAY_LORA = 96
ICL_LORA = 96
GATE_LORA = 256
W_DECAY_SCALE = 0.606531
RWKV_GN_EPS = 64e-5

GLA_HEADS = 4
GLA_VDIM = D_MODEL // 2
GLA_KDIM = GLA_VDIM // 2
GLA_DK = GLA_KDIM // GLA_HEADS
GLA_DV = GLA_VDIM // GLA_HEADS
GLA_GATE_LORA = 16
GLA_GATE_NORMALIZER = 16.0

DIFF_HEAD_DIM = 64
DIFF_V_DIM = 2 * DIFF_HEAD_DIM
DIFF_HEADS = D_MODEL // DIFF_V_DIM
ROPE_THETA = 10000.0

D_FF = -(-(8 * D_MODEL) // (3 * 256)) * 256

LANES = 128
VMEM_LIMIT = 56 * 1024 * 1024

CHUNK = 64
ROPE_SUB_ROWS = 256
LN_SUB_ROWS = 128
GLA_SUB_ROWS = 256
LORA_PAD = 128
HALO_ROWS = 16
Z_SEGS = {
    "rkv": (0, 3 * RWKV_DIM, 0),
    "wd": (3 * RWKV_DIM, DECAY_LORA, 3072),
    "ad": (3 * RWKV_DIM + DECAY_LORA, ICL_LORA, 3200),
    "gd": (3 * RWKV_DIM + DECAY_LORA + ICL_LORA, GATE_LORA, 3328),
    "q": (3520, GLA_KDIM, 3584),
    "k": (3520 + GLA_KDIM, GLA_KDIM, 4096),
    "v": (3520 + 2 * GLA_KDIM, GLA_VDIM, 4608),
    "gg": (3520 + 2 * GLA_KDIM + GLA_VDIM, GLA_GATE_LORA, 6656),
    "og": (3520 + 2 * GLA_KDIM + GLA_VDIM + GLA_GATE_LORA, GLA_VDIM, 5632),
}
Z_A_COLS = 3584
Z_COLS = 7168


def _cparams(sem):
    return pltpu.CompilerParams(dimension_semantics=sem, vmem_limit_bytes=VMEM_LIMIT)


def _dot(a, b, prec=None):
    return jnp.dot(a, b, preferred_element_type=F32, precision=prec)


def _dot_nt(a, b, prec=None):
    return lax.dot_general(a, b, (((1,), (1,)), ((), ())), preferred_element_type=F32,
                           precision=prec)


def _dot_tn(a, b, prec=None):
    return lax.dot_general(a, b, (((0,), (0,)), ((), ())), preferred_element_type=F32,
                           precision=prec)


def _sigmoid(x):
    return 1.0 / (1.0 + jnp.exp(-x))


def _layer_norm_rows(u, g, b, eps):
    mu = jnp.mean(u, axis=-1, keepdims=True)
    d = u - mu
    var = jnp.mean(d * d, axis=-1, keepdims=True)
    return d * lax.rsqrt(var + eps) * g + b


def _ada_body(c_ref, w_ref, b_ref, o_ref):
    c = c_ref[...]
    sc = (c * _sigmoid(c)).astype(BF16)
    o_ref[0] = _dot(sc, w_ref[0].astype(BF16)) + b_ref[0]


def ada_modulation(c, ada_w, ada_b):
    B, D = c.shape
    n = ada_w.shape[0] * ada_w.shape[1]
    w = ada_w.reshape(n, D, 3 * D)
    b = ada_b.reshape(n, 1, 3 * D)
    rows = 8
    c_pad = jnp.pad(c, ((0, rows - B), (0, 0)))
    tn = 1024
    out = pl.pallas_call(
        _ada_body,
        grid=(n, 3 * D // tn),
        in_specs=[
            pl.BlockSpec((rows, D), lambda i, j: (0, 0)),
            pl.BlockSpec((1, D, tn), lambda i, j: (i, 0, j)),
            pl.BlockSpec((1, 1, tn), lambda i, j: (i, 0, j)),
        ],
        out_specs=pl.BlockSpec((1, rows, tn), lambda i, j: (i, 0, j)),
        out_shape=jax.ShapeDtypeStruct((n, rows, 3 * D), F32),
        compiler_params=_cparams(("parallel", "parallel")),
        name="ada_modulation",
    )(c_pad, w, b)
    mods = out[:, :B, :]
    shift, scale, gate = mods[..., :D], mods[..., D:2 * D], mods[..., 2 * D:]
    r3 = lambda t: t.reshape(n, B, 1, D)
    return r3(shift), r3(scale), r3(gate)


def _inproj_body(x_ref, sh_ref, sc_ref, w_ref, o_ref, h_ref):
    @pl.when(pl.program_id(2) == 0)
    def _():
        h_ref[...] = (x_ref[0] * (1.0 + sc_ref[0]) + sh_ref[0]).astype(BF16)

    o_ref[0] = _dot(h_ref[...], w_ref[...]).astype(o_ref.dtype)


def modulated_projection(x, shift, scale, w, out_dtype, tm, tn):
    B, T, D = x.shape
    N = w.shape[1]
    return pl.pallas_call(
        _inproj_body,
        grid=(B, T // tm, N // tn),
        in_specs=[
            pl.BlockSpec((1, tm, D), lambda b, i, j: (b, i, 0)),
            pl.BlockSpec((1, 1, D), lambda b, i, j: (b, 0, 0)),
            pl.BlockSpec((1, 1, D), lambda b, i, j: (b, 0, 0)),
            pl.BlockSpec((D, tn), lambda b, i, j: (0, j)),
        ],
        out_specs=pl.BlockSpec((1, tm, tn), lambda b, i, j: (b, i, j)),
        out_shape=jax.ShapeDtypeStruct((B, T, N), out_dtype),
        scratch_shapes=[pltpu.VMEM((tm, D), BF16)],
        compiler_params=_cparams(("parallel", "parallel", "arbitrary")),
        name="modulated_projection",
    )(x, shift, scale, w)


def _qkv_body(x_ref, sh_ref, sc_ref, w_ref, cos_ref, sin_ref, o_ref, h_ref, *, n_q, n_qk,
              q_scale):
    j = pl.program_id(2)

    @pl.when(j == 0)
    def _():
        h_ref[...] = (x_ref[0] * (1.0 + sc_ref[0]) + sh_ref[0]).astype(BF16)

    is_qk = j < n_qk
    mult = jnp.where(j < n_q, q_scale, 1.0).astype(F32)
    cos = jnp.where(is_qk, cos_ref[...] * mult, 1.0)
    sin = jnp.where(is_qk, sin_ref[...] * mult, 0.0)
    tm, tn = o_ref.shape[1], o_ref.shape[2]
    sub = min(ROPE_SUB_ROWS, tm)
    blocks = [slice(s * sub, (s + 1) * sub) for s in range(tm // sub)]
    accs = [_dot(h_ref[rows, :], w_ref[...]) for rows in blocks]
    for rows, acc in zip(blocks, accs):
        for s in range(tn // LANES):
            xs = acc[:, s * LANES:(s + 1) * LANES]
            rot = pltpu.roll(xs, LANES // 2, axis=1)
            o_ref[0, rows, s * LANES:(s + 1) * LANES] = (xs * cos[rows] + rot * sin[rows]).astype(o_ref.dtype)


def qkv_projection(x, shift, scale, w, cos_t, sin_t, tm, tn):
    B, T, D = x.shape
    N = w.shape[1]
    n_q = D_MODEL // tn
    body = functools.partial(_qkv_body, n_q=n_q, n_qk=2 * n_q,
                             q_scale=DIFF_HEAD_DIM ** -0.5 * math.log2(math.e))
    return pl.pallas_call(
        body,
        grid=(B, T // tm, N // tn),
        in_specs=[
            pl.BlockSpec((1, tm, D), lambda b, i, j: (b, i, 0)),
            pl.BlockSpec((1, 1, D), lambda b, i, j: (b, 0, 0)),
            pl.BlockSpec((1, 1, D), lambda b, i, j: (b, 0, 0)),
            pl.BlockSpec((D, tn), lambda b, i, j: (0, j)),
            pl.BlockSpec((tm, LANES), lambda b, i, j: (i, 0)),
            pl.BlockSpec((tm, LANES), lambda b, i, j: (i, 0)),
        ],
        out_specs=pl.BlockSpec((1, tm, tn), lambda b, i, j: (b, i, j)),
        out_shape=jax.ShapeDtypeStruct((B, T, N), BF16),
        scratch_shapes=[pltpu.VMEM((tm, D), BF16)],
        compiler_params=_cparams(("parallel", "parallel", "arbitrary")),
        name="qkv_projection",
    )(x, shift, scale, w, cos_t, sin_t)


def _proj_ln_body(*refs, n_parts):
    y_refs = refs[:n_parts]
    w_ref, x_ref, gate_ref, g_ref, b_ref, o_ref = refs[n_parts:]
    tm = x_ref.shape[1]
    sub = min(LN_SUB_ROWS, tm)
    accs = []
    for s in range(tm // sub):
        rows = slice(s * sub, (s + 1) * sub)
        acc = None
        off = 0
        for yr in y_refs:
            k = yr.shape[-1]
            part = _dot(yr[0, rows, :], w_ref[off:off + k, :])
            acc = part if acc is None else acc + part
            off += k
        accs.append(acc)
    for s, acc in enumerate(accs):
        rows = slice(s * sub, (s + 1) * sub)
        u = ALPHA * x_ref[0, rows, :] + (1.0 + gate_ref[0]) * acc
        o_ref[0, rows, :] = _layer_norm_rows(u, g_ref[...], b_ref[...], LN_EPS)


def projection_layernorm(y_parts, w, x, gate, ln_g, ln_b, tm):
    B, T, D = x.shape
    K = w.shape[0]
    in_specs = [pl.BlockSpec((1, tm, yp.shape[-1]), lambda b, i: (b, i, 0)) for yp in y_parts]
    in_specs += [
        pl.BlockSpec((K, D), lambda b, i: (0, 0)),
        pl.BlockSpec((1, tm, D), lambda b, i: (b, i, 0)),
        pl.BlockSpec((1, 1, D), lambda b, i: (b, 0, 0)),
        pl.BlockSpec((1, D), lambda b, i: (0, 0)),
        pl.BlockSpec((1, D), lambda b, i: (0, 0)),
    ]
    return pl.pallas_call(
        functools.partial(_proj_ln_body, n_parts=len(y_parts)),
        grid=(B, T // tm),
        in_specs=in_specs,
        out_specs=pl.BlockSpec((1, tm, D), lambda b, i: (b, i, 0)),
        out_shape=jax.ShapeDtypeStruct((B, T, D), F32),
        compiler_params=_cparams(("parallel", "parallel")),
        name="projection_layernorm",
    )(*y_parts, w, x, gate, ln_g.reshape(1, D), ln_b.reshape(1, D))


def _ffn_body(x_ref, sh_ref, sc_ref, gate_ref, wg_ref, wu_ref, wo_ref, g_ref, b_ref, o_ref,
              h_ref, acc_ref):
    j = pl.program_id(2)

    @pl.when(j == 0)
    def _():
        h_ref[...] = (x_ref[0] * (1.0 + sc_ref[0]) + sh_ref[0]).astype(BF16)
        acc_ref[...] = jnp.zeros_like(acc_ref)

    h = h_ref[...]
    gt = _dot(h, wg_ref[0])
    up = _dot(h, wu_ref[0])
    act = (gt * _sigmoid(gt) * up).astype(BF16)
    acc_ref[...] += _dot(act, wo_ref[0])

    @pl.when(j == pl.num_programs(2) - 1)
    def _():
        u = ALPHA * x_ref[0] + (1.0 + gate_ref[0]) * acc_ref[...]
        o_ref[0] = _layer_norm_rows(u, g_ref[...], b_ref[...], LN_EPS)


def ffn_sublayer(x, shift, scale, gate, w_in, w_out, layer, ln_g, ln_b, tm, tf):
    B, T, D = x.shape
    F = w_out.shape[1]
    nf = F // tf
    return pl.pallas_call(
        _ffn_body,
        grid=(B, T // tm, nf),
        in_specs=[
            pl.BlockSpec((1, tm, D), lambda b, i, j: (b, i, 0)),
            pl.BlockSpec((1, 1, D), lambda b, i, j: (b, 0, 0)),
            pl.BlockSpec((1, 1, D), lambda b, i, j: (b, 0, 0)),
            pl.BlockSpec((1, 1, D), lambda b, i, j: (b, 0, 0)),
            pl.BlockSpec((1, D, tf), lambda b, i, j: (layer, 0, j)),
            pl.BlockSpec((1, D, tf), lambda b, i, j: (layer, 0, j + nf)),
            pl.BlockSpec((1, tf, D), lambda b, i, j: (layer, j, 0)),
            pl.BlockSpec((1, D), lambda b, i, j: (0, 0)),
            pl.BlockSpec((1, D), lambda b, i, j: (0, 0)),
        ],
        out_specs=pl.BlockSpec((1, tm, D), lambda b, i, j: (b, i, 0)),
        out_shape=jax.ShapeDtypeStruct((B, T, D), F32),
        scratch_shapes=[pltpu.VMEM((tm, D), BF16), pltpu.VMEM((tm, D), F32)],
        compiler_params=_cparams(("parallel", "parallel", "arbitrary")),
        name="ffn_sublayer",
    )(x, shift, scale, gate, w_in, w_in, w_out, ln_g.reshape(1, D), ln_b.reshape(1, D))


ONES_ROWS = 16


KEY_CHUNK = 512


def _attn_body(q0_ref, qn_ref, kn_ref, v_ref, lam_ref, g_ref, o_ref, vt_ref, s_ref, m_ref,
               *, lambda_init):
    dv = DIFF_V_DIM
    T = kn_ref.shape[1]
    first_tile = pl.program_id(2) == 0
    first_step = first_tile & (pl.program_id(0) == 0) & (pl.program_id(1) == 0)
    lane = lax.broadcasted_iota(jnp.int32, (1, LANES), 1)
    comp0 = (lane // (DIFF_HEAD_DIM // 2)) % 2 == 0

    def comps(q):
        zero = jnp.zeros_like(q)
        return (jnp.where(comp0, q, zero), jnp.where(comp0, zero, q))

    def chunk_scores(qcs, rows):
        kc = kn_ref[0, rows, :]
        return [_dot_nt(kc, qc) for qc in qcs]

    def col_max(s):
        return jnp.max(s, axis=0, keepdims=True)

    chunks = [slice(c * KEY_CHUNK, (c + 1) * KEY_CHUNK) for c in range(T // KEY_CHUNK)]

    @pl.when(first_tile)
    def _():
        vt_ref[0:dv, :] = v_ref[0].astype(F32).T.astype(BF16)
        vt_ref[dv:, :] = jnp.ones((ONES_ROWS, T), BF16)

    @pl.when(first_step)
    def _():
        qcs = comps(q0_ref[0])
        m = [None, None]
        for rows in chunks:
            s = chunk_scores(qcs, rows)
            for j in range(2):
                s_ref[j, rows, :] = s[j]
                m[j] = col_max(s[j]) if m[j] is None else jnp.maximum(m[j], col_max(s[j]))
        for j in range(2):
            m_ref[j] = m[j]

    qcs = comps(qn_ref[0])
    m_cur = [m_ref[0], m_ref[1]]
    m_new = [None, None]
    acc = [None, None]
    for rows in chunks:
        es = [jnp.exp2(s_ref[j, rows, :] - m_cur[j]).astype(BF16) for j in range(2)]
        for j in range(2):
            part = _dot(vt_ref[:, rows], es[j])
            acc[j] = part if acc[j] is None else acc[j] + part
        s_new = chunk_scores(qcs, rows)
        for j in range(2):
            s_ref[j, rows, :] = s_new[j]
            m_new[j] = col_max(s_new[j]) if m_new[j] is None else jnp.maximum(m_new[j], col_max(s_new[j]))
    for j in range(2):
        m_ref[j] = m_new[j]

    lp = lam_ref[...]
    lam = (jnp.exp(jnp.sum(lp[0:1] * lp[1:2], axis=-1, keepdims=True))
           - jnp.exp(jnp.sum(lp[2:3] * lp[3:4], axis=-1, keepdims=True)) + lambda_init)
    outs = [a[0:dv] / a[dv:dv + 1] for a in acc]
    o = (outs[0] - lam * outs[1]).T
    ms = jnp.mean(o * o, axis=-1, keepdims=True)
    o = o * lax.rsqrt(ms + LN_EPS) * g_ref[...] * (1.0 - lambda_init)
    o_ref[0] = o.astype(o_ref.dtype)


def diff_attention(qkv, lam_params, subln_g, lambda_init, tq):
    B, T, _ = qkv.shape
    H = DIFF_HEADS
    nq = T // tq

    def nxt(b, h, i):
        lin = jnp.minimum((b * H + h) * nq + i + 1, B * H * nq - 1)
        return lin // (H * nq), (lin // nq) % H, lin % nq

    def qn_map(b, h, i):
        nb, nh, nt = nxt(b, h, i)
        return nb, nt, nh

    def kn_map(b, h, i):
        nb, nh, _ = nxt(b, h, i)
        return nb, 0, H + nh

    return pl.pallas_call(
        functools.partial(_attn_body, lambda_init=lambda_init),
        grid=(B, H, nq),
        in_specs=[
            pl.BlockSpec((1, tq, LANES), lambda b, h, i: (0, 0, 0)),
            pl.BlockSpec((1, tq, LANES), qn_map),
            pl.BlockSpec((1, T, LANES), kn_map),
            pl.BlockSpec((1, T, LANES), lambda b, h, i: (b, 0, 2 * H + h)),
            pl.BlockSpec((4, DIFF_HEAD_DIM), lambda b, h, i: (0, 0)),
            pl.BlockSpec((1, LANES), lambda b, h, i: (0, 0)),
        ],
        out_specs=pl.BlockSpec((1, tq, LANES), lambda b, h, i: (b, i, h)),
        out_shape=jax.ShapeDtypeStruct((B, T, H * DIFF_V_DIM), BF16),
        scratch_shapes=[pltpu.VMEM((DIFF_V_DIM + ONES_ROWS, T), BF16),
                        pltpu.VMEM((2, T, tq), F32),
                        pltpu.VMEM((2, 1, tq), F32)],
        compiler_params=_cparams(("arbitrary", "arbitrary", "arbitrary")),
        name="diff_attention",
    )(qkv, qkv, qkv, qkv, lam_params, subln_g.reshape(1, LANES))


def _per_head_sum(x, head_dim):
    lane = lax.broadcasted_iota(jnp.int32, (1, LANES), 1)
    lo = lane < head_dim
    s_lo = jnp.sum(jnp.where(lo, x, 0.0), axis=-1, keepdims=True)
    s_hi = jnp.sum(jnp.where(lo, 0.0, x), axis=-1, keepdims=True)
    return jnp.where(lo, s_lo, s_hi)


def _rwkv_pre_body(z_ref, zp_ref, zn_ref, mu_ref, w0_ref, wup_ref, a0_ref, aup_ref, gup_ref,
                   kk_ref, ka_ref, r_o, k_o, v_o, kk_o, ba_o, lw_o, g_o, *, tm):
    i = pl.program_id(1)
    last = pl.num_programs(1) - 1
    row8 = lax.broadcasted_iota(jnp.int32, (8, 1), 0)
    C = RWKV_DIM

    def shifted(lo, hi):
        z = z_ref[0, :, lo:hi].astype(F32)
        prev = jnp.where(i == 0, 0.0, zp_ref[0, HALO_ROWS - 1:HALO_ROWS, lo:hi].astype(F32))
        nxt = jnp.where(i == last, 0.0, zn_ref[0, 0:1, lo:hi].astype(F32))
        nb = pltpu.roll(z, 1, axis=0) + pltpu.roll(z, tm - 1, axis=0)
        top = nb[0:8] + jnp.where(row8 == 0, prev - z[tm - 1:tm], 0.0)
        bot = nb[tm - 8:tm] + jnp.where(row8 == 7, nxt - z[0:1], 0.0)
        nb = jnp.concatenate([top, nb[8:tm - 8], bot], axis=0)
        mu = mu_ref[:, lo:hi]
        return z * (1.0 - mu) + nb * (0.5 * mu)

    r = shifted(0, C)
    k = shifted(C, 2 * C)
    v = shifted(2 * C, 3 * C)
    wd = shifted(3 * C, 3 * C + LORA_PAD)
    ad = shifted(3 * C + LORA_PAD, 3 * C + 2 * LORA_PAD)
    gd = shifted(3 * C + 2 * LORA_PAD, 3 * C + 2 * LORA_PAD + GATE_LORA)

    r_o[0] = r.astype(r_o.dtype)
    v_o[0] = v.astype(v_o.dtype)
    twd = jnp.tanh(wd).astype(BF16)
    for d in range(2):
        lw_o[d, 0] = -W_DECAY_SCALE * _sigmoid(w0_ref[d] + _dot(twd, wup_ref[d].astype(BF16)))
    a = _sigmoid(a0_ref[...] + _dot(ad.astype(BF16), aup_ref[...].astype(BF16)))
    g_o[0] = _dot(_sigmoid(gd).astype(BF16), gup_ref[...].astype(BF16)).astype(g_o.dtype)
    k_o[0] = (k * (1.0 + (a - 1.0) * ka_ref[...])).astype(k_o.dtype)
    kk0 = k * kk_ref[...]
    for s in range(C // LANES):
        sl = slice(s * LANES, (s + 1) * LANES)
        x = kk0[:, sl]
        nrm = jnp.maximum(jnp.sqrt(_per_head_sum(x * x, RWKV_HEAD_DIM)), 1e-12)
        kk = x / nrm
        kk_o[0, :, sl] = kk.astype(kk_o.dtype)
        ba_o[0, :, sl] = (kk * a[:, sl]).astype(ba_o.dtype)


def rwkv_pre(z, mu_p, w0, wup_p, a0, aup_p, gup, k_k, k_a, tm):
    B, T, _ = z.shape
    C = RWKV_DIM
    nbh = tm // HALO_ROWS
    nh = T // HALO_ROWS
    bt = jax.ShapeDtypeStruct((B, T, C), BF16)
    vec = lambda t: t.reshape(1, C)
    full = lambda shp: pl.BlockSpec(shp, lambda b, i: (0,) * len(shp))
    body = functools.partial(_rwkv_pre_body, tm=tm)
    return pl.pallas_call(
        body,
        grid=(B, T // tm),
        in_specs=[
            pl.BlockSpec((1, tm, Z_A_COLS), lambda b, i: (b, i, 0)),
            pl.BlockSpec((1, HALO_ROWS, Z_A_COLS), lambda b, i: (b, jnp.maximum(i * nbh - 1, 0), 0)),
            pl.BlockSpec((1, HALO_ROWS, Z_A_COLS),
                         lambda b, i: (b, jnp.minimum((i + 1) * nbh, nh - 1), 0)),
            full((1, Z_A_COLS)),
            full((2, 1, C)),
            full((2, LORA_PAD, C)),
            full((1, C)),
            full((LORA_PAD, C)),
            full((GATE_LORA, C)),
            full((1, C)),
            full((1, C)),
        ],
        out_specs=[pl.BlockSpec((1, tm, C), lambda b, i: (b, i, 0))] * 5
        + [pl.BlockSpec((2, 1, tm, C), lambda b, i: (0, b, i, 0)),
           pl.BlockSpec((1, tm, C), lambda b, i: (b, i, 0))],
        out_shape=[bt] * 5 + [jax.ShapeDtypeStruct((2, B, T, C), F32), bt],
        compiler_params=_cparams(("parallel", "parallel")),
        name="rwkv_pre",
    )(z, z, z, mu_p, w0.reshape(2, 1, C), wup_p, vec(a0), aup_p, gup, vec(k_k), vec(k_a))


def _bf(x):
    return x.astype(BF16)


def _segmented_cumsum(x, seg):
    pos = lax.broadcasted_iota(jnp.int32, (x.shape[0], 1), 0) % seg
    s = 1
    while s < seg:
        x = x + jnp.where(pos >= s, pltpu.roll(x, s, axis=0), 0.0)
        s *= 2
    return x


def _segment_totals(x, seg):
    parts = []
    for c in range(x.shape[0] // seg):
        t = jnp.sum(x[c * seg:(c + 1) * seg], axis=0, keepdims=True)
        parts.append(jnp.broadcast_to(t, (seg, x.shape[1])))
    return jnp.concatenate(parts, axis=0)


def _decay_factors(lw, reverse):
    tot = _segment_totals(lw, CHUNK)
    cum = _segmented_cumsum(lw, CHUNK)
    if reverse:
        cum = tot - cum + lw
    return cum, tot


def _rwkv_chunks_local(chunks):
    L = CHUNK
    lane = lax.broadcasted_iota(jnp.int32, (1, LANES), 1)
    h0 = lane < RWKV_HEAD_DIM

    def stack(x):
        return jnp.concatenate([jnp.where(h0, x, 0.0), jnp.where(h0, 0.0, x)], axis=0)

    pre = []
    for r, k, v, kk, ba, lw, cum, tot, strict, incl in chunks:
        g_ex = jnp.exp(cum - lw)
        g_inv = jnp.exp(-cum)
        g_rem = jnp.exp(tot - cum)
        a_st = stack(-kk * g_ex)
        r_st = stack(r * jnp.exp(cum))
        lhs = _bf(jnp.concatenate([a_st, r_st], axis=0))
        rhs = _bf(jnp.concatenate([stack(ba * g_inv), stack(k * g_inv)], axis=0))
        t_b = _bf(jnp.concatenate([stack(ba * g_rem), stack(k * g_rem)], axis=0))
        pre.append((a_st, r_st, lhs, rhs, t_b, _bf(stack(v))))

    ps = [_dot_nt(lhs, rhs) for _, _, lhs, rhs, _, _ in pre]
    m_ab, m_ak, m_r = [], [], []
    for p, ch in zip(ps, chunks):
        strict, incl = ch[8], ch[9]
        m_ab.append(jnp.where(strict, p[:2 * L, :2 * L], 0.0))
        m_ak.append(_bf(jnp.where(strict, p[:2 * L, 2 * L:], 0.0)))
        m_r.append(_bf(jnp.concatenate([jnp.where(incl, p[2 * L:, :2 * L], 0.0),
                                        jnp.where(incl, p[2 * L:, 2 * L:], 0.0)], axis=1)))

    akv = [_dot(m, pr[5]) for m, pr in zip(m_ak, pre)]
    xs = [jnp.concatenate([pr[0], t], axis=1) for pr, t in zip(pre, akv)]
    ns = m_ab
    steps = int(math.log2(L))
    for it in range(steps):
        n_bs = [_bf(n) for n in ns]
        xs = [x + _dot(n_b, _bf(x)) for x, n_b in zip(xs, n_bs)]
        if it + 1 < steps:
            ns = [_dot(n_b, n_b) for n_b in n_bs]

    out = []
    wu = [(_bf(x[:, :LANES]), _bf(x[:, LANES:])) for x in xs]
    qys = [_dot(m, jnp.concatenate([jnp.concatenate([w_b, ul_b], axis=1),
                                    jnp.concatenate([jnp.zeros_like(pr[5]), pr[5]], axis=1)], axis=0))
           for m, (w_b, ul_b), pr in zip(m_r, wu, pre)]
    gps = [_dot_tn(w_b, pr[4][:2 * L]) for (w_b, _), pr in zip(wu, pre)]
    hs = [_dot_tn(jnp.concatenate([ul_b, pr[5]], axis=0), pr[4])
          for (_, ul_b), pr in zip(wu, pre)]
    for qy, gp, h, pr in zip(qys, gps, hs, pre):
        out.append((_bf(pr[1] + qy[:, :LANES]), qy[:, LANES:], _bf(gp), h))
    return out


def _rwkv_scan_body(rf, kf, vf, kkf, baf, lwf, rb, kb, vb, kkb, bab, lwb, yf_ref, yb_ref, s_ref,
                    *, n_chunks):
    L = CHUNK

    @pl.when(pl.program_id(2) == 0)
    def _():
        s_ref[...] = jnp.zeros_like(s_ref)

    r2 = lax.broadcasted_iota(jnp.int32, (2 * L, 2 * L), 0)
    c2 = lax.broadcasted_iota(jnp.int32, (2 * L, 2 * L), 1)
    same_head = (r2 // L) == (c2 // L)
    dt = r2 % L - c2 % L

    chunks, g_tots = [], []
    for d, refs in ((0, (rf, kf, vf, kkf, baf, lwf)), (1, (rb, kb, vb, kkb, bab, lwb))):
        r_ref, k_ref, v_ref, kk_ref, ba_ref, lw_ref = refs
        strict = jnp.where(same_head, dt if d == 0 else -dt, -1) > 0
        incl = strict | (r2 == c2)
        lw_all = lw_ref[0, 0]
        cum_all, tot_all = _decay_factors(lw_all, reverse=(d == 1))
        g_tot_all = jnp.exp(tot_all)
        for c in range(n_chunks):
            rows = slice(c * L, (c + 1) * L)
            chunks.append((r_ref[0, rows, :], k_ref[0, rows, :], v_ref[0, rows, :],
                           kk_ref[0, rows, :], ba_ref[0, rows, :], lw_all[rows], cum_all[rows],
                           tot_all[rows], strict, incl))
            g_tots.append(g_tot_all[c * L:c * L + 1])
    local = _rwkv_chunks_local(chunks)

    s2 = [s_ref[0], s_ref[1]]
    for j in range(n_chunks):
        for d, y_ref in ((0, yf_ref), (1, yb_ref)):
            c = j if d == 0 else n_chunks - 1 - j
            q_b, yl_st, gp_b, h = local[d * n_chunks + c]
            s_b = _bf(s2[d])
            y_st = _dot_nt(q_b, s_b) + yl_st
            y_ref[0, c * L:(c + 1) * L, :] = y_st[:L] + y_st[L:]
            s2[d] = s2[d] * g_tots[d * n_chunks + c] + _dot(s_b, gp_b) + h
    s_ref[0] = s2[0]
    s_ref[1] = s2[1]


def rwkv_scan(r, k, v, kk, ba, lw, tl):
    B, T, C = r.shape
    n_pairs = C // LANES
    nblk = T // tl
    fwd = pl.BlockSpec((1, tl, LANES), lambda b, p, i: (b, i, p))
    bwd = pl.BlockSpec((1, tl, LANES), lambda b, p, i: (b, nblk - 1 - i, p))
    lw_f = pl.BlockSpec((1, 1, tl, LANES), lambda b, p, i: (0, b, i, p))
    lw_b = pl.BlockSpec((1, 1, tl, LANES), lambda b, p, i: (1, b, nblk - 1 - i, p))
    y_sds = jax.ShapeDtypeStruct((B, T, C), F32)
    return pl.pallas_call(
        functools.partial(_rwkv_scan_body, n_chunks=tl // CHUNK),
        grid=(B, n_pairs, nblk),
        in_specs=[fwd] * 5 + [lw_f] + [bwd] * 5 + [lw_b],
        out_specs=[fwd, bwd],
        out_shape=[y_sds, y_sds],
        scratch_shapes=[pltpu.VMEM((2, LANES, LANES), F32)],
        compiler_params=_cparams(("parallel", "parallel", "arbitrary")),
        name="rwkv_scan",
    )(r, k, v, kk, ba, lw, r, k, v, kk, ba, lw)


def _rwkv_post_body(yf_ref, yb_ref, r_ref, k_ref, v_ref, g_ref, rk_ref, gg_ref, gb_ref, o_ref):
    C = RWKV_DIM
    for s in range(C // LANES):
        sl = slice(s * LANES, (s + 1) * LANES)
        y = yf_ref[0, :, sl] + yb_ref[0, :, sl]
        inv_n = 1.0 / RWKV_HEAD_DIM
        mu = _per_head_sum(y, RWKV_HEAD_DIM) * inv_n
        dlt = y - mu
        var = _per_head_sum(dlt * dlt, RWKV_HEAD_DIM) * inv_n
        yn = dlt * lax.rsqrt(var + RWKV_GN_EPS) * gg_ref[:, sl] + gb_ref[:, sl]
        rk = r_ref[0, :, sl].astype(F32) * k_ref[0, :, sl].astype(F32)
        bonus = _per_head_sum(rk * rk_ref[:, sl], RWKV_HEAD_DIM)
        out = (yn + bonus * v_ref[0, :, sl].astype(F32)) * g_ref[0, :, sl].astype(F32)
        o_ref[0, :, sl] = out.astype(o_ref.dtype)


def rwkv_post(yf, yb, r, k, v, g, r_k, gn_g, gn_b, tm):
    B, T, C = r.shape
    spec = pl.BlockSpec((1, tm, C), lambda b, i: (b, i, 0))
    vspec = pl.BlockSpec((1, C), lambda b, i: (0, 0))
    return pl.pallas_call(
        _rwkv_post_body,
        grid=(B, T // tm),
        in_specs=[spec] * 6 + [vspec] * 3,
        out_specs=spec,
        out_shape=jax.ShapeDtypeStruct((B, T, C), BF16),
        compiler_params=_cparams(("parallel", "parallel")),
        name="rwkv_post",
    )(yf, yb, r, k, v, g, r_k.reshape(1, C), gn_g.reshape(1, C), gn_b.reshape(1, C))


def _log_sigmoid(x):
    return jnp.minimum(x, 0.0) - jnp.log(1.0 + jnp.exp(-jnp.abs(x)))


def _gla_body(qf, kf, vf, ggf, qb_ref, kb_ref, vb_ref, ggb, gup_ref, gb_ref, of_ref, ob_ref, st_ref,
              *, n_chunks):
    L = CHUNK
    tl = n_chunks * L

    @pl.when(pl.program_id(2) == 0)
    def _():
        st_ref[...] = jnp.zeros_like(st_ref)

    sub = min(GLA_SUB_ROWS, tl)
    ri = lax.broadcasted_iota(jnp.int32, (sub, sub), 0)
    ci = lax.broadcasted_iota(jnp.int32, (sub, sub), 1)
    same_chunk = (ri // L) == (ci // L)
    chunk_rows = [slice(c * L, (c + 1) * L) for c in range(n_chunks)]

    loc = []
    for d, (q_ref, k_ref, v_ref, gg_ref) in enumerate(((qf, kf, vf, ggf), (qb_ref, kb_ref, vb_ref, ggb))):
        lg = _log_sigmoid(_dot(_bf(gg_ref[0]), _bf(gup_ref[d])) + gb_ref[d]) * (1.0 / GLA_GATE_NORMALIZER)
        cum, tot = _decay_factors(lg, reverse=(d == 1))
        k = k_ref[0].astype(F32)
        loc.append(dict(
            v=_bf(v_ref[0]),
            qb=_bf(q_ref[0].astype(F32) * (GLA_DK ** -0.5) * jnp.exp(cum)),
            kb=_bf(k * jnp.exp(-cum)),
            kt=_bf(k * jnp.exp(tot - cum)),
            g_tot=jnp.exp(tot),
            incl=jnp.where(same_chunk, (ri - ci) if d == 0 else (ci - ri), -1) >= 0))
    blocks = [slice(s * sub, (s + 1) * sub) for s in range(tl // sub)]
    atts = [[jnp.where(lc["incl"], _dot_nt(lc["qb"][b], lc["kb"][b]), 0.0) for b in blocks] for lc in loc]
    o_intra = [jnp.concatenate([_dot(_bf(a), lc["v"][b]) for a, b in zip(att, blocks)], axis=0)
               for att, lc in zip(atts, loc)]
    kvs = [[_dot_tn(lc["v"][r], lc["kt"][r]) for r in chunk_rows] for lc in loc]

    st = [st_ref[0], st_ref[1]]
    for j in range(n_chunks):
        for d, o_ref in ((0, of_ref), (1, ob_ref)):
            c = j if d == 0 else n_chunks - 1 - j
            rows = chunk_rows[c]
            o_ref[0, rows, :] = o_intra[d][rows] + _dot_nt(loc[d]["qb"][rows], _bf(st[d]))
            st[d] = st[d] * loc[d]["g_tot"][c * L:c * L + 1] + kvs[d][c]
    st_ref[0] = st[0]
    st_ref[1] = st[1]


def gla_scan(z, gate_up_p, gate_b, tl):
    B, T, _ = z.shape
    H = GLA_HEADS
    nblk = T // tl
    qc = Z_SEGS["q"][2] // GLA_DK
    kc = Z_SEGS["k"][2] // GLA_DK
    vc = Z_SEGS["v"][2] // GLA_DV
    gc = Z_SEGS["gg"][2] // LANES
    specs = []
    for tmap in (lambda i: i, lambda i: nblk - 1 - i):
        specs += [
            pl.BlockSpec((1, tl, GLA_DK), lambda b, h, i, tmap=tmap: (b, tmap(i), qc + h)),
            pl.BlockSpec((1, tl, GLA_DK), lambda b, h, i, tmap=tmap: (b, tmap(i), kc + h)),
            pl.BlockSpec((1, tl, GLA_DV), lambda b, h, i, tmap=tmap: (b, tmap(i), vc + h)),
            pl.BlockSpec((1, tl, LANES), lambda b, h, i, tmap=tmap: (b, tmap(i), gc)),
        ]
    o_sds = jax.ShapeDtypeStruct((B, T, GLA_VDIM), F32)
    return pl.pallas_call(
        functools.partial(_gla_body, n_chunks=tl // CHUNK),
        grid=(B, H, nblk),
        in_specs=specs + [
            pl.BlockSpec((2, LANES, GLA_DK), lambda b, h, i: (0, 0, h)),
            pl.BlockSpec((2, 1, GLA_DK), lambda b, h, i: (0, 0, h)),
        ],
        out_specs=[pl.BlockSpec((1, tl, GLA_DV), lambda b, h, i: (b, i, h)),
                   pl.BlockSpec((1, tl, GLA_DV), lambda b, h, i: (b, nblk - 1 - i, h))],
        out_shape=[o_sds, o_sds],
        scratch_shapes=[pltpu.VMEM((2, GLA_DV, GLA_DK), F32)],
        compiler_params=_cparams(("parallel", "parallel", "arbitrary")),
        name="gla_scan",
    )(z, z, z, z, z, z, z, z, gate_up_p, gate_b.reshape(2, 1, GLA_KDIM))


def _gla_post_body(of_ref, ob_ref, og_ref, g_ref, y_ref):
    o = of_ref[0] + ob_ref[0]
    ms = jnp.mean(o * o, axis=-1, keepdims=True)
    on = o * lax.rsqrt(ms + LN_EPS) * g_ref[...]
    og = og_ref[0].astype(F32)
    y_ref[0] = (on * (og * _sigmoid(og))).astype(y_ref.dtype)


def gla_post(o_f, o_b, z, norm_g, tm):
    B, T, _ = o_f.shape
    H = GLA_HEADS
    oc = Z_SEGS["og"][2] // GLA_DV
    return pl.pallas_call(
        _gla_post_body,
        grid=(B, T // tm, H),
        in_specs=[
            pl.BlockSpec((1, tm, GLA_DV), lambda b, i, h: (b, i, h)),
            pl.BlockSpec((1, tm, GLA_DV), lambda b, i, h: (b, i, h)),
            pl.BlockSpec((1, tm, GLA_DV), lambda b, i, h: (b, i, oc + h)),
            pl.BlockSpec((1, GLA_DV), lambda b, i, h: (0, 0)),
        ],
        out_specs=pl.BlockSpec((1, tm, GLA_DV), lambda b, i, h: (b, i, h)),
        out_shape=jax.ShapeDtypeStruct((B, T, GLA_VDIM), BF16),
        compiler_params=_cparams(("parallel", "parallel", "parallel")),
        name="gla_post",
    )(o_f, o_b, z, norm_g.reshape(1, GLA_DV))


def _pad_ab_columns(w):
    parts, col = [], 0
    for o_start, width, n_start in sorted(Z_SEGS.values(), key=lambda seg: seg[2]):
        if n_start > col:
            parts.append(jnp.zeros((w.shape[0], n_start - col), w.dtype))
        parts.append(w[:, o_start:o_start + width])
        col = n_start + width
    parts.append(jnp.zeros((w.shape[0], Z_COLS - col), w.dtype))
    return jnp.concatenate(parts, axis=1)


def _pad_rows(w, rows):
    pad = [(0, 0)] * w.ndim
    pad[-2] = (0, rows - w.shape[-2])
    return jnp.pad(w, pad)


def _permute_qk_columns(w):
    half = DIFF_HEAD_DIM // 2
    w5 = w.reshape(w.shape[0], DIFF_HEADS, 2, 2, half)
    return w5.transpose(0, 1, 3, 2, 4).reshape(w.shape)


def _rope_tables(T):
    half = DIFF_HEAD_DIM // 2
    inv = ROPE_THETA ** (-jnp.arange(0, DIFF_HEAD_DIM, 2, dtype=F32) / DIFF_HEAD_DIM)
    ang = jnp.arange(T, dtype=F32)[:, None] * inv[None, :]
    cos, sin = jnp.cos(ang), jnp.sin(ang)
    cos_t = jnp.tile(cos, (1, 4))
    sin_t = jnp.concatenate([-sin, -sin, sin, sin], axis=1)
    return cos_t, sin_t


def _lambda_init(layer):
    return 0.8 - 0.6 * math.exp(-0.3 * layer)


def _mixer_ab(x, shift, scale, ab_w_in, rwkv_mu, rwkv_w0, rwkv_w_up, rwkv_a0, rwkv_a_up,
              rwkv_g_up, rwkv_k_k, rwkv_k_a, rwkv_r_k, rwkv_gn_g, rwkv_gn_b,
              gla_gate_up, gla_gate_b, gla_norm_g, blocks):
    w_p = _pad_ab_columns(ab_w_in.astype(BF16))
    z = modulated_projection(x, shift, scale, w_p, BF16, blocks["proj_tm"], 1024)

    mu_p = jnp.zeros((1, Z_A_COLS), F32)
    for name in ("rkv", "wd", "ad", "gd"):
        o_start, width, n_start = Z_SEGS[name]
        mu_p = lax.dynamic_update_slice(mu_p, rwkv_mu[None, o_start:o_start + width], (0, n_start))
    r, k, v, kk, ba, lw, g = rwkv_pre(
        z, mu_p, rwkv_w0, _pad_rows(rwkv_w_up, LORA_PAD), rwkv_a0, _pad_rows(rwkv_a_up, LORA_PAD),
        rwkv_g_up, rwkv_k_k, rwkv_k_a, blocks["pre_tm"])
    y_f, y_r = rwkv_scan(r, k, v, kk, ba, lw, blocks["rwkv_tl"])
    y_a = rwkv_post(y_f, y_r, r, k, v, g, rwkv_r_k, rwkv_gn_g, rwkv_gn_b, blocks["pre_tm"])

    o_f, o_r = gla_scan(z, _pad_rows(gla_gate_up, LANES), gla_gate_b, blocks["scan_tl"])
    y_b = gla_post(o_f, o_r, z, gla_norm_g, blocks["post_tm"])
    return y_a, y_b


def _mixer_c(x, shift, scale, diff_w_in, diff_lambda, diff_subln_g, lambda_init, blocks):
    T = x.shape[1]
    w_b = diff_w_in.astype(BF16)
    w = jnp.concatenate([_permute_qk_columns(w_b[:, :D_MODEL]),
                         _permute_qk_columns(w_b[:, D_MODEL:2 * D_MODEL]),
                         w_b[:, 2 * D_MODEL:]], axis=1)
    cos_t, sin_t = _rope_tables(T)
    qkv = qkv_projection(x, shift, scale, w, cos_t, sin_t, blocks["proj_tm"], 1024)
    return diff_attention(qkv, diff_lambda, diff_subln_g, lambda_init, blocks["attn_tq"])


def _blocks(T):
    return {
        "proj_tm": min(1024, T),
        "pre_tm": min(256, T),
        "post_tm": min(1024, T),
        "rwkv_tl": min(512, T),
        "scan_tl": min(512, T),
        "attn_tq": min(256, T),
        "ln_tm": min(512, T),
        "ffn_tm": min(512, T),
    }


def kernel(x, c, ada_w, ada_b, ln_g, ln_b, ffn_w_in, ffn_w_out, ab_w_in, ab_w_out, rwkv_mu, rwkv_w0, rwkv_w_up, rwkv_a0, rwkv_a_up, rwkv_g_up, rwkv_k_k, rwkv_k_a, rwkv_r_k, rwkv_gn_g, rwkv_gn_b, gla_gate_up, gla_gate_b, gla_norm_g, diff_w_in, diff_w_out, diff_lambda, diff_subln_g):
    blocks = _blocks(x.shape[1])
    shift, scale, gate = ada_modulation(c, ada_w, ada_b)
    ffn_w_in_b = ffn_w_in.astype(BF16)
    ffn_w_out_b = ffn_w_out.astype(BF16)
    for i in range(DEPTH):
        j = i // 2
        m = 2 * i
        if i % 2 == 0:
            y_parts = _mixer_ab(
                x, shift[m], scale[m], ab_w_in[j], rwkv_mu[j], rwkv_w0[j], rwkv_w_up[j],
                rwkv_a0[j], rwkv_a_up[j], rwkv_g_up[j], rwkv_k_k[j], rwkv_k_a[j], rwkv_r_k[j],
                rwkv_gn_g[j], rwkv_gn_b[j], gla_gate_up[j], gla_gate_b[j], gla_norm_g[j], blocks)
            w_out = ab_w_out[j]
        else:
            y_parts = (_mixer_c(x, shift[m], scale[m], diff_w_in[j], diff_lambda[j],
                                diff_subln_g[j], _lambda_init(i), blocks),)
            w_out = diff_w_out[j]
        x = projection_layernorm(y_parts, w_out.astype(BF16), x, gate[m], ln_g[i, 0], ln_b[i, 0],
                                 blocks["ln_tm"])
        x = ffn_sublayer(x, shift[m + 1], scale[m + 1], gate[m + 1], ffn_w_in_b, ffn_w_out_b, i,
                         ln_g[i, 1], ln_b[i, 1], blocks["ffn_tm"], 512)
    return x
```

```python
import functools
import math

import jax
import jax.numpy as jnp
from jax import lax
from jax.experimental import pallas as pl
from jax.experimental.pallas import tpu as pltpu

F32 = jnp.float32
BF16 = jnp.bfloat16
HI = lax.Precision.HIGHEST

D_MODEL = 2048
DEPTH = 2
ALPHA = (2.0 * DEPTH) ** 0.25
LN_EPS = 1e-5

RWKV_HEAD_DIM = 64
RWKV_DIM = D_MODEL // 2
RWKV_HEADS = RWKV_DIM // RWKV_HEAD_DIM
DECAY_LORA = 96
ICL_LORA = 96
GATE_LORA = 256
W_DECAY_SCALE = 0.606531
RWKV_GN_EPS = 64e-5

GLA_HEADS = 4
GLA_VDIM = D_MODEL // 2
GLA_KDIM = GLA_VDIM // 2
GLA_DK = GLA_KDIM // GLA_HEADS
GLA_DV = GLA_VDIM // GLA_HEADS
GLA_GATE_LORA = 16
GLA_GATE_NORMALIZER = 16.0

DIFF_HEAD_DIM = 64
DIFF_V_DIM = 2 * DIFF_HEAD_DIM
DIFF_HEADS = D_MODEL // DIFF_V_DIM
ROPE_THETA = 10000.0

D_FF = -(-(8 * D_MODEL) // (3 * 256)) * 256

LANES = 128
VMEM_LIMIT = 56 * 1024 * 1024

CHUNK = 64
ROPE_SUB_ROWS = 256
FFN_SUB_ROWS = 256
LN_SUB_ROWS = 128
GLA_SUB_ROWS = 256
LORA_PAD = 128
HALO_ROWS = 16
Z_SEGS = {
    "rkv": (0, 3 * RWKV_DIM, 0),
    "wd": (3 * RWKV_DIM, DECAY_LORA, 3072),
    "ad": (3 * RWKV_DIM + DECAY_LORA, ICL_LORA, 3200),
    "gd": (3 * RWKV_DIM + DECAY_LORA + ICL_LORA, GATE_LORA, 3328),
    "q": (3520, GLA_KDIM, 3584),
    "k": (3520 + GLA_KDIM, GLA_KDIM, 4096),
    "v": (3520 + 2 * GLA_KDIM, GLA_VDIM, 4608),
    "gg": (3520 + 2 * GLA_KDIM + GLA_VDIM, GLA_GATE_LORA, 6656),
    "og": (3520 + 2 * GLA_KDIM + GLA_VDIM + GLA_GATE_LORA, GLA_VDIM, 5632),
}
Z_A_COLS = 3584
Z_COLS = 7168


def _cparams(sem):
    return pltpu.CompilerParams(dimension_semantics=sem, vmem_limit_bytes=VMEM_LIMIT)


def _dot(a, b, prec=None):
    return jnp.dot(a, b, preferred_element_type=F32, precision=prec)


def _dot_nt(a, b, prec=None):
    return lax.dot_general(a, b, (((1,), (1,)), ((), ())), preferred_element_type=F32,
                           precision=prec)


def _dot_tn(a, b, prec=None):
    return lax.dot_general(a, b, (((0,), (0,)), ((), ())), preferred_element_type=F32,
                           precision=prec)


def _sigmoid(x):
    return 1.0 / (1.0 + jnp.exp(-x))


def _layer_norm_rows(u, g, b, eps):
    mu = jnp.mean(u, axis=-1, keepdims=True)
    d = u - mu
    var = jnp.mean(d * d, axis=-1, keepdims=True)
    return d * lax.rsqrt(var + eps) * g + b


def _ada_body(c_ref, w_ref, b_ref, o_ref):
    c = c_ref[...]
    sc = (c * _sigmoid(c)).astype(BF16)
    o_ref[0] = _dot(sc, w_ref[0].astype(BF16)) + b_ref[0]


def ada_modulation(c, ada_w, ada_b):
    B, D = c.shape
    n = ada_w.shape[0] * ada_w.shape[1]
    w = ada_w.reshape(n, D, 3 * D)
    b = ada_b.reshape(n, 1, 3 * D)
    rows = 8
    c_pad = jnp.pad(c, ((0, rows - B), (0, 0)))
    tn = 1024
    out = pl.pallas_call(
        _ada_body,
        grid=(n, 3 * D // tn),
        in_specs=[
            pl.BlockSpec((rows, D), lambda i, j: (0, 0)),
            pl.BlockSpec((1, D, tn), lambda i, j: (i, 0, j)),
            pl.BlockSpec((1, 1, tn), lambda i, j: (i, 0, j)),
        ],
        out_specs=pl.BlockSpec((1, rows, tn), lambda i, j: (i, 0, j)),
        out_shape=jax.ShapeDtypeStruct((n, rows, 3 * D), F32),
        compiler_params=_cparams(("parallel", "parallel")),
        name="ada_modulation",
    )(c_pad, w, b)
    mods = out[:, :B, :]
    shift, scale, gate = mods[..., :D], mods[..., D:2 * D], mods[..., 2 * D:]
    r3 = lambda t: t.reshape(n, B, 1, D)
    return r3(shift), r3(scale), r3(gate)


def _inproj_body(x_ref, sh_ref, sc_ref, w_ref, o_ref, h_ref):
    @pl.when(pl.program_id(2) == 0)
    def _():
        h_ref[...] = (x_ref[0] * (1.0 + sc_ref[0]) + sh_ref[0]).astype(BF16)

    o_ref[0] = _dot(h_ref[...], w_ref[...]).astype(o_ref.dtype)


def modulated_projection(x, shift, scale, w, out_dtype, tm, tn):
    B, T, D = x.shape
    N = w.shape[1]
    return pl.pallas_call(
        _inproj_body,
        grid=(B, T // tm, N // tn),
        in_specs=[
            pl.BlockSpec((1, tm, D), lambda b, i, j: (b, i, 0)),
            pl.BlockSpec((1, 1, D), lambda b, i, j: (b, 0, 0)),
            pl.BlockSpec((1, 1, D), lambda b, i, j: (b, 0, 0)),
            pl.BlockSpec((D, tn), lambda b, i, j: (0, j)),
        ],
        out_specs=pl.BlockSpec((1, tm, tn), lambda b, i, j: (b, i, j)),
        out_shape=jax.ShapeDtypeStruct((B, T, N), out_dtype),
        scratch_shapes=[pltpu.VMEM((tm, D), BF16)],
        compiler_params=_cparams(("parallel", "parallel", "arbitrary")),
        name="modulated_projection",
    )(x, shift, scale, w)


def _qkv_body(x_ref, sh_ref, sc_ref, w_ref, cos_ref, sup_ref, sdn_ref, o_ref, h_ref, *, n_q, n_qk,
              q_scale):
    j = pl.program_id(2)

    @pl.when(j == 0)
    def _():
        h_ref[...] = (x_ref[0] * (1.0 + sc_ref[0]) + sh_ref[0]).astype(BF16)

    is_qk = j < n_qk
    mult = jnp.where(j < n_q, q_scale, 1.0).astype(F32)
    cos = jnp.where(is_qk, cos_ref[...] * mult, 1.0)
    sup = jnp.where(is_qk, sup_ref[...] * mult, 0.0)
    sdn = jnp.where(is_qk, sdn_ref[...] * mult, 0.0)
    half = DIFF_HEAD_DIM // 2
    tm, tn = o_ref.shape[1], o_ref.shape[2]
    sub = min(ROPE_SUB_ROWS, tm)
    blocks = [slice(s * sub, (s + 1) * sub) for s in range(tm // sub)]
    accs = [_dot(h_ref[rows, :], w_ref[...]) for rows in blocks]
    for rows, acc in zip(blocks, accs):
        for s in range(tn // LANES):
            xs = acc[:, s * LANES:(s + 1) * LANES]
            up = pltpu.roll(xs, LANES - half, axis=1)
            dn = pltpu.roll(xs, half, axis=1)
            o_ref[0, rows, s * LANES:(s + 1) * LANES] = (
                xs * cos[rows] + up * sup[rows] + dn * sdn[rows]).astype(o_ref.dtype)


def qkv_projection(x, shift, scale, w, rope_tables, tm, tn):
    B, T, D = x.shape
    N = w.shape[1]
    n_q = D_MODEL // tn
    body = functools.partial(_qkv_body, n_q=n_q, n_qk=2 * n_q,
                             q_scale=DIFF_HEAD_DIM ** -0.5 * math.log2(math.e))
    return pl.pallas_call(
        body,
        grid=(B, T // tm, N // tn),
        in_specs=[
            pl.BlockSpec((1, tm, D), lambda b, i, j: (b, i, 0)),
            pl.BlockSpec((1, 1, D), lambda b, i, j: (b, 0, 0)),
            pl.BlockSpec((1, 1, D), lambda b, i, j: (b, 0, 0)),
            pl.BlockSpec((D, tn), lambda b, i, j: (0, j)),
            pl.BlockSpec((tm, LANES), lambda b, i, j: (i, 0)),
            pl.BlockSpec((tm, LANES), lambda b, i, j: (i, 0)),
            pl.BlockSpec((tm, LANES), lambda b, i, j: (i, 0)),
        ],
        out_specs=pl.BlockSpec((1, tm, tn), lambda b, i, j: (b, i, j)),
        out_shape=jax.ShapeDtypeStruct((B, T, N), BF16),
        scratch_shapes=[pltpu.VMEM((tm, D), BF16)],
        compiler_params=_cparams(("parallel", "parallel", "arbitrary")),
        name="qkv_projection",
    )(x, shift, scale, w, *rope_tables)


def _proj_ln_body(*refs, n_parts):
    y_refs = refs[:n_parts]
    w_ref, x_ref, gate_ref, g_ref, b_ref, o_ref = refs[n_parts:]
    tm = x_ref.shape[1]
    sub = min(LN_SUB_ROWS, tm)
    accs = []
    for s in range(tm // sub):
        rows = slice(s * sub, (s + 1) * sub)
        acc = None
        off = 0
        for yr in y_refs:
            k = yr.shape[-1]
            part = _dot(yr[0, rows, :], w_ref[off:off + k, :])
            acc = part if acc is None else acc + part
            off += k
        accs.append(acc)
    for s, acc in enumerate(accs):
        rows = slice(s * sub, (s + 1) * sub)
        u = ALPHA * x_ref[0, rows, :] + (1.0 + gate_ref[0]) * acc
        o_ref[0, rows, :] = _layer_norm_rows(u, g_ref[...], b_ref[...], LN_EPS)


def projection_layernorm(y_parts, w, x, gate, ln_g, ln_b, tm):
    B, T, D = x.shape
    K = w.shape[0]
    in_specs = [pl.BlockSpec((1, tm, yp.shape[-1]), lambda b, i: (b, i, 0)) for yp in y_parts]
    in_specs += [
        pl.BlockSpec((K, D), lambda b, i: (0, 0)),
        pl.BlockSpec((1, tm, D), lambda b, i: (b, i, 0)),
        pl.BlockSpec((1, 1, D), lambda b, i: (b, 0, 0)),
        pl.BlockSpec((1, D), lambda b, i: (0, 0)),
        pl.BlockSpec((1, D), lambda b, i: (0, 0)),
    ]
    return pl.pallas_call(
        functools.partial(_proj_ln_body, n_parts=len(y_parts)),
        grid=(B, T // tm),
        in_specs=in_specs,
        out_specs=pl.BlockSpec((1, tm, D), lambda b, i: (b, i, 0)),
        out_shape=jax.ShapeDtypeStruct((B, T, D), F32),
        compiler_params=_cparams(("parallel", "parallel")),
        name="projection_layernorm",
    )(*y_parts, w, x, gate, ln_g.reshape(1, D), ln_b.reshape(1, D))


def _ffn_body(x_ref, sh_ref, sc_ref, gate_ref, wg_ref, wu_ref, wo_ref, g_ref, b_ref, o_ref,
              h_ref, acc_ref):
    j = pl.program_id(2)
    last = pl.num_programs(2) - 1
    tm = x_ref.shape[1]
    sub = min(FFN_SUB_ROWS, tm)
    blocks = [slice(s * sub, (s + 1) * sub) for s in range(tm // sub)]

    def contribution(h):
        gt = _dot(h, wg_ref[0])
        up = _dot(h, wu_ref[0])
        act = (gt * _sigmoid(gt) * up).astype(BF16)
        return _dot(act, wo_ref[0])

    @pl.when(j == 0)
    def _():
        for rows in blocks:
            h = (x_ref[0, rows, :] * (1.0 + sc_ref[0]) + sh_ref[0]).astype(BF16)
            h_ref[rows, :] = h
            acc_ref[rows, :] = contribution(h)

    @pl.when((j > 0) & (j < last))
    def _():
        acc_ref[...] += contribution(h_ref[...])

    @pl.when(j == last)
    def _():
        for rows in blocks:
            acc = acc_ref[rows, :] + contribution(h_ref[rows, :])
            u = ALPHA * x_ref[0, rows, :] + (1.0 + gate_ref[0]) * acc
            o_ref[0, rows, :] = _layer_norm_rows(u, g_ref[...], b_ref[...], LN_EPS)


def ffn_sublayer(x, shift, scale, gate, w_in, w_out, layer, ln_g, ln_b, tm, tf):
    B, T, D = x.shape
    F = w_out.shape[1]
    nf = F // tf
    assert nf >= 2, "the first and last F-steps must be distinct grid steps"
    return pl.pallas_call(
        _ffn_body,
        grid=(B, T // tm, nf),
        in_specs=[
            pl.BlockSpec((1, tm, D), lambda b, i, j: (b, i, 0)),
            pl.BlockSpec((1, 1, D), lambda b, i, j: (b, 0, 0)),
            pl.BlockSpec((1, 1, D), lambda b, i, j: (b, 0, 0)),
            pl.BlockSpec((1, 1, D), lambda b, i, j: (b, 0, 0)),
            pl.BlockSpec((1, D, tf), lambda b, i, j: (layer, 0, j)),
            pl.BlockSpec((1, D, tf), lambda b, i, j: (layer, 0, j + nf)),
            pl.BlockSpec((1, tf, D), lambda b, i, j: (layer, j, 0)),
            pl.BlockSpec((1, D), lambda b, i, j: (0, 0)),
            pl.BlockSpec((1, D), lambda b, i, j: (0, 0)),
        ],
        out_specs=pl.BlockSpec((1, tm, D), lambda b, i, j: (b, i, 0)),
        out_shape=jax.ShapeDtypeStruct((B, T, D), F32),
        scratch_shapes=[pltpu.VMEM((tm, D), BF16), pltpu.VMEM((tm, D), F32)],
        compiler_params=_cparams(("parallel", "parallel", "arbitrary")),
        name="ffn_sublayer",
    )(x, shift, scale, gate, w_in, w_in, w_out, ln_g.reshape(1, D), ln_b.reshape(1, D))


ONES_ROWS = 16


KEY_CHUNK = 512


def _attn_body(q0_ref, qn_ref, kn_ref, v_ref, lam_ref, g_ref, o_ref, vt_ref, s_ref, m_ref,
               *, lambda_init):
    dv = DIFF_V_DIM
    T = kn_ref.shape[1]
    first_tile = pl.program_id(2) == 0
    first_step = first_tile & (pl.program_id(0) == 0) & (pl.program_id(1) == 0)
    lane = lax.broadcasted_iota(jnp.int32, (1, LANES), 1)
    comp0 = lane < DIFF_HEAD_DIM

    def comps(q):
        zero = jnp.zeros_like(q)
        return (jnp.where(comp0, q, zero), jnp.where(comp0, zero, q))

    def chunk_scores(qcs, rows):
        kc = kn_ref[0, rows, :]
        return [_dot_nt(kc, qc) for qc in qcs]

    def col_max(s):
        return jnp.max(s, axis=0, keepdims=True)

    chunks = [slice(c * KEY_CHUNK, (c + 1) * KEY_CHUNK) for c in range(T // KEY_CHUNK)]

    @pl.when(first_tile)
    def _():
        vt_ref[0:dv, :] = v_ref[0].astype(F32).T.astype(BF16)
        vt_ref[dv:, :] = jnp.ones((ONES_ROWS, T), BF16)

    @pl.when(first_step)
    def _():
        qcs = comps(q0_ref[0])
        m = [None, None]
        for rows in chunks:
            s = chunk_scores(qcs, rows)
            for j in range(2):
                s_ref[j, rows, :] = s[j]
                m[j] = col_max(s[j]) if m[j] is None else jnp.maximum(m[j], col_max(s[j]))
        for j in range(2):
            m_ref[j] = m[j]

    qcs = comps(qn_ref[0])
    m_cur = [m_ref[0], m_ref[1]]
    m_new = [None, None]
    acc = [None, None]
    for rows in chunks:
        es = [jnp.exp2(s_ref[j, rows, :] - m_cur[j]).astype(BF16) for j in range(2)]
        for j in range(2):
            part = _dot(vt_ref[:, rows], es[j])
            acc[j] = part if acc[j] is None else acc[j] + part
        s_new = chunk_scores(qcs, rows)
        for j in range(2):
            s_ref[j, rows, :] = s_new[j]
            m_new[j] = col_max(s_new[j]) if m_new[j] is None else jnp.maximum(m_new[j], col_max(s_new[j]))
    for j in range(2):
        m_ref[j] = m_new[j]

    lp = lam_ref[...]
    lam = (jnp.exp(jnp.sum(lp[0:1] * lp[1:2], axis=-1, keepdims=True))
           - jnp.exp(jnp.sum(lp[2:3] * lp[3:4], axis=-1, keepdims=True)) + lambda_init)
    outs = [a[0:dv] / a[dv:dv + 1] for a in acc]
    o = (outs[0] - lam * outs[1]).T
    ms = jnp.mean(o * o, axis=-1, keepdims=True)
    o = o * lax.rsqrt(ms + LN_EPS) * g_ref[...] * (1.0 - lambda_init)
    o_ref[0] = o.astype(o_ref.dtype)


def diff_attention(qkv, lam_params, subln_g, lambda_init, tq):
    B, T, _ = qkv.shape
    H = DIFF_HEADS
    nq = T // tq

    def nxt(b, h, i):
        lin = jnp.minimum((b * H + h) * nq + i + 1, B * H * nq - 1)
        return lin // (H * nq), (lin // nq) % H, lin % nq

    def qn_map(b, h, i):
        nb, nh, nt = nxt(b, h, i)
        return nb, nt, nh

    def kn_map(b, h, i):
        nb, nh, _ = nxt(b, h, i)
        return nb, 0, H + nh

    return pl.pallas_call(
        functools.partial(_attn_body, lambda_init=lambda_init),
        grid=(B, H, nq),
        in_specs=[
            pl.BlockSpec((1, tq, LANES), lambda b, h, i: (0, 0, 0)),
            pl.BlockSpec((1, tq, LANES), qn_map),
            pl.BlockSpec((1, T, LANES), kn_map),
            pl.BlockSpec((1, T, LANES), lambda b, h, i: (b, 0, 2 * H + h)),
            pl.BlockSpec((4, DIFF_HEAD_DIM), lambda b, h, i: (0, 0)),
            pl.BlockSpec((1, LANES), lambda b, h, i: (0, 0)),
        ],
        out_specs=pl.BlockSpec((1, tq, LANES), lambda b, h, i: (b, i, h)),
        out_shape=jax.ShapeDtypeStruct((B, T, H * DIFF_V_DIM), BF16),
        scratch_shapes=[pltpu.VMEM((DIFF_V_DIM + ONES_ROWS, T), BF16),
                        pltpu.VMEM((2, T, tq), F32),
                        pltpu.VMEM((2, 1, tq), F32)],
        compiler_params=_cparams(("arbitrary", "arbitrary", "arbitrary")),
        name="diff_attention",
    )(qkv, qkv, qkv, qkv, lam_params, subln_g.reshape(1, LANES))


def _per_head_sum(x, head_dim):
    lane = lax.broadcasted_iota(jnp.int32, (1, LANES), 1)
    lo = lane < head_dim
    s_lo = jnp.sum(jnp.where(lo, x, 0.0), axis=-1, keepdims=True)
    s_hi = jnp.sum(jnp.where(lo, 0.0, x), axis=-1, keepdims=True)
    return jnp.where(lo, s_lo, s_hi)


def _rwkv_pre_body(z_ref, zp_ref, zn_ref, mu_ref, w0_ref, wup_ref, a0_ref, aup_ref, gup_ref,
                   kk_ref, ka_ref, r_o, k_o, v_o, kk_o, ba_o, lw_o, g_o, *, tm):
    i = pl.program_id(1)
    last = pl.num_programs(1) - 1
    row8 = lax.broadcasted_iota(jnp.int32, (8, 1), 0)
    C = RWKV_DIM

    def shifted(lo, hi):
        z = z_ref[0, :, lo:hi].astype(F32)
        prev = jnp.where(i == 0, 0.0, zp_ref[0, HALO_ROWS - 1:HALO_ROWS, lo:hi].astype(F32))
        nxt = jnp.where(i == last, 0.0, zn_ref[0, 0:1, lo:hi].astype(F32))
        nb = pltpu.roll(z, 1, axis=0) + pltpu.roll(z, tm - 1, axis=0)
        top = nb[0:8] + jnp.where(row8 == 0, prev - z[tm - 1:tm], 0.0)
        bot = nb[tm - 8:tm] + jnp.where(row8 == 7, nxt - z[0:1], 0.0)
        nb = jnp.concatenate([top, nb[8:tm - 8], bot], axis=0)
        mu = mu_ref[:, lo:hi]
        return z * (1.0 - mu) + nb * (0.5 * mu)

    r = shifted(0, C)
    k = shifted(C, 2 * C)
    v = shifted(2 * C, 3 * C)
    wd = shifted(3 * C, 3 * C + LORA_PAD)
    ad = shifted(3 * C + LORA_PAD, 3 * C + 2 * LORA_PAD)
    gd = shifted(3 * C + 2 * LORA_PAD, 3 * C + 2 * LORA_PAD + GATE_LORA)

    r_o[0] = r.astype(r_o.dtype)
    v_o[0] = v.astype(v_o.dtype)
    twd = jnp.tanh(wd).astype(BF16)
    for d in range(2):
        lw_o[d, 0] = -W_DECAY_SCALE * _sigmoid(w0_ref[d] + _dot(twd, wup_ref[d].astype(BF16)))
    a = _sigmoid(a0_ref[...] + _dot(ad.astype(BF16), aup_ref[...].astype(BF16)))
    g_o[0] = _dot(_sigmoid(gd).astype(BF16), gup_ref[...].astype(BF16)).astype(g_o.dtype)
    k_o[0] = (k * (1.0 + (a - 1.0) * ka_ref[...])).astype(k_o.dtype)
    kk0 = k * kk_ref[...]
    for s in range(C // LANES):
        sl = slice(s * LANES, (s + 1) * LANES)
        x = kk0[:, sl]
        nrm = jnp.maximum(jnp.sqrt(_per_head_sum(x * x, RWKV_HEAD_DIM)), 1e-12)
        kk = x / nrm
        kk_o[0, :, sl] = kk.astype(kk_o.dtype)
        ba_o[0, :, sl] = (kk * a[:, sl]).astype(ba_o.dtype)


def rwkv_pre(z, mu_p, w0, wup_p, a0, aup_p, gup, k_k, k_a, tm):
    B, T, _ = z.shape
    C = RWKV_DIM
    nbh = tm // HALO_ROWS
    nh = T // HALO_ROWS
    bt = jax.ShapeDtypeStruct((B, T, C), BF16)
    vec = lambda t: t.reshape(1, C)
    full = lambda shp: pl.BlockSpec(shp, lambda b, i: (0,) * len(shp))
    body = functools.partial(_rwkv_pre_body, tm=tm)
    return pl.pallas_call(
        body,
        grid=(B, T // tm),
        in_specs=[
            pl.BlockSpec((1, tm, Z_A_COLS), lambda b, i: (b, i, 0)),
            pl.BlockSpec((1, HALO_ROWS, Z_A_COLS), lambda b, i: (b, jnp.maximum(i * nbh - 1, 0), 0)),
            pl.BlockSpec((1, HALO_ROWS, Z_A_COLS),
                         lambda b, i: (b, jnp.minimum((i + 1) * nbh, nh - 1), 0)),
            full((1, Z_A_COLS)),
            full((2, 1, C)),
            full((2, LORA_PAD, C)),
            full((1, C)),
            full((LORA_PAD, C)),
            full((GATE_LORA, C)),
            full((1, C)),
            full((1, C)),
        ],
        out_specs=[pl.BlockSpec((1, tm, C), lambda b, i: (b, i, 0))] * 5
        + [pl.BlockSpec((2, 1, tm, C), lambda b, i: (0, b, i, 0)),
           pl.BlockSpec((1, tm, C), lambda b, i: (b, i, 0))],
        out_shape=[bt] * 5 + [jax.ShapeDtypeStruct((2, B, T, C), F32), bt],
        compiler_params=_cparams(("parallel", "parallel")),
        name="rwkv_pre",
    )(z, z, z, mu_p, w0.reshape(2, 1, C), wup_p, vec(a0), aup_p, gup, vec(k_k), vec(k_a))


def _bf(x):
    return x.astype(BF16)


def _segmented_cumsum(x, seg):
    pos = lax.broadcasted_iota(jnp.int32, (x.shape[0], 1), 0) % seg
    s = 1
    while s < seg:
        x = x + jnp.where(pos >= s, pltpu.roll(x, s, axis=0), 0.0)
        s *= 2
    return x


def _segment_totals(x, seg):
    parts = []
    for c in range(x.shape[0] // seg):
        t = jnp.sum(x[c * seg:(c + 1) * seg], axis=0, keepdims=True)
        parts.append(jnp.broadcast_to(t, (seg, x.shape[1])))
    return jnp.concatenate(parts, axis=0)


def _decay_factors(lw, reverse):
    tot = _segment_totals(lw, CHUNK)
    cum = _segmented_cumsum(lw, CHUNK)
    if reverse:
        cum = tot - cum + lw
    return cum, tot


def _rwkv_chunks_local(chunks):
    L = CHUNK
    lane = lax.broadcasted_iota(jnp.int32, (1, LANES), 1)
    h0 = lane < RWKV_HEAD_DIM

    def stack(x):
        return jnp.concatenate([jnp.where(h0, x, 0.0), jnp.where(h0, 0.0, x)], axis=0)

    pre = []
    for r, k, v, kk, ba, lw, cum, tot, strict, incl in chunks:
        g_ex = jnp.exp(cum - lw)
        g_inv = jnp.exp(-cum)
        g_rem = jnp.exp(tot - cum)
        a_st = stack(-kk * g_ex)
        r_st = stack(r * jnp.exp(cum))
        lhs = _bf(jnp.concatenate([a_st, r_st], axis=0))
        rhs = _bf(jnp.concatenate([stack(ba * g_inv), stack(k * g_inv)], axis=0))
        t_b = _bf(jnp.concatenate([stack(ba * g_rem), stack(k * g_rem)], axis=0))
        pre.append((a_st, r_st, lhs, rhs, t_b, _bf(stack(v))))

    ps = [_dot_nt(lhs, rhs) for _, _, lhs, rhs, _, _ in pre]
    m_ab, m_ak, m_r = [], [], []
    for p, ch in zip(ps, chunks):
        strict, incl = ch[8], ch[9]
        m_ab.append(jnp.where(strict, p[:2 * L, :2 * L], 0.0))
        m_ak.append(_bf(jnp.where(strict, p[:2 * L, 2 * L:], 0.0)))
        m_r.append(_bf(jnp.concatenate([jnp.where(incl, p[2 * L:, :2 * L], 0.0),
                                        jnp.where(incl, p[2 * L:, 2 * L:], 0.0)], axis=1)))

    akv = [_dot(m, pr[5]) for m, pr in zip(m_ak, pre)]
    xs = [jnp.concatenate([pr[0], t], axis=1) for pr, t in zip(pre, akv)]
    ns = m_ab
    steps = int(math.log2(L))
    for it in range(steps):
        n_bs = [_bf(n) for n in ns]
        xs = [x + _dot(n_b, _bf(x)) for x, n_b in zip(xs, n_bs)]
        if it + 1 < steps:
            ns = [_dot(n_b, n_b) for n_b in n_bs]

    out = []
    wu = [(_bf(x[:, :LANES]), _bf(x[:, LANES:])) for x in xs]
    qys = [_dot(m, jnp.concatenate([jnp.concatenate([w_b, ul_b], axis=1),
                                    jnp.concatenate([jnp.zeros_like(pr[5]), pr[5]], axis=1)], axis=0))
           for m, (w_b, ul_b), pr in zip(m_r, wu, pre)]
    gps = [_dot_tn(w_b, pr[4][:2 * L]) for (w_b, _), pr in zip(wu, pre)]
    hs = [_dot_tn(jnp.concatenate([ul_b, pr[5]], axis=0), pr[4])
          for (_, ul_b), pr in zip(wu, pre)]
    for qy, gp, h, pr in zip(qys, gps, hs, pre):
        out.append((_bf(pr[1] + qy[:, :LANES]), qy[:, LANES:], _bf(gp), h))
    return out


def _rwkv_scan_body(rf, kf, vf, kkf, baf, lwf, rb, kb, vb, kkb, bab, lwb, yf_ref, yb_ref, s_ref,
                    *, n_chunks):
    L = CHUNK

    @pl.when(pl.program_id(2) == 0)
    def _():
        s_ref[...] = jnp.zeros_like(s_ref)

    r2 = lax.broadcasted_iota(jnp.int32, (2 * L, 2 * L), 0)
    c2 = lax.broadcasted_iota(jnp.int32, (2 * L, 2 * L), 1)
    same_head = (r2 // L) == (c2 // L)
    dt = r2 % L - c2 % L

    chunks, g_tots = [], []
    for d, refs in ((0, (rf, kf, vf, kkf, baf, lwf)), (1, (rb, kb, vb, kkb, bab, lwb))):
        r_ref, k_ref, v_ref, kk_ref, ba_ref, lw_ref = refs
        strict = jnp.where(same_head, dt if d == 0 else -dt, -1) > 0
        incl = strict | (r2 == c2)
        lw_all = lw_ref[0, 0]
        cum_all, tot_all = _decay_factors(lw_all, reverse=(d == 1))
        g_tot_all = jnp.exp(tot_all)
        for c in range(n_chunks):
            rows = slice(c * L, (c + 1) * L)
            chunks.append((r_ref[0, rows, :], k_ref[0, rows, :], v_ref[0, rows, :],
                           kk_ref[0, rows, :], ba_ref[0, rows, :], lw_all[rows], cum_all[rows],
                           tot_all[rows], strict, incl))
            g_tots.append(g_tot_all[c * L:c * L + 1])
    local = _rwkv_chunks_local(chunks)

    s2 = [s_ref[0], s_ref[1]]
    for j in range(n_chunks):
        for d, y_ref in ((0, yf_ref), (1, yb_ref)):
            c = j if d == 0 else n_chunks - 1 - j
            q_b, yl_st, gp_b, h = local[d * n_chunks + c]
            s_b = _bf(s2[d])
            y_st = _dot_nt(q_b, s_b) + yl_st
            y_ref[0, c * L:(c + 1) * L, :] = y_st[:L] + y_st[L:]
            s2[d] = s2[d] * g_tots[d * n_chunks + c] + _dot(s_b, gp_b) + h
    s_ref[0] = s2[0]
    s_ref[1] = s2[1]


def rwkv_scan(r, k, v, kk, ba, lw, tl):
    B, T, C = r.shape
    n_pairs = C // LANES
    nblk = T // tl
    fwd = pl.BlockSpec((1, tl, LANES), lambda b, p, i: (b, i, p))
    bwd = pl.BlockSpec((1, tl, LANES), lambda b, p, i: (b, nblk - 1 - i, p))
    lw_f = pl.BlockSpec((1, 1, tl, LANES), lambda b, p, i: (0, b, i, p))
    lw_b = pl.BlockSpec((1, 1, tl, LANES), lambda b, p, i: (1, b, nblk - 1 - i, p))
    y_sds = jax.ShapeDtypeStruct((B, T, C), F32)
    return pl.pallas_call(
        functools.partial(_rwkv_scan_body, n_chunks=tl // CHUNK),
        grid=(B, n_pairs, nblk),
        in_specs=[fwd] * 5 + [lw_f] + [bwd] * 5 + [lw_b],
        out_specs=[fwd, bwd],
        out_shape=[y_sds, y_sds],
        scratch_shapes=[pltpu.VMEM((2, LANES, LANES), F32)],
        compiler_params=_cparams(("parallel", "parallel", "arbitrary")),
        name="rwkv_scan",
    )(r, k, v, kk, ba, lw, r, k, v, kk, ba, lw)


def _rwkv_post_body(yf_ref, yb_ref, r_ref, k_ref, v_ref, g_ref, rk_ref, gg_ref, gb_ref, o_ref):
    C = RWKV_DIM
    for s in range(C // LANES):
        sl = slice(s * LANES, (s + 1) * LANES)
        y = yf_ref[0, :, sl] + yb_ref[0, :, sl]
        inv_n = 1.0 / RWKV_HEAD_DIM
        mu = _per_head_sum(y, RWKV_HEAD_DIM) * inv_n
        dlt = y - mu
        var = _per_head_sum(dlt * dlt, RWKV_HEAD_DIM) * inv_n
        yn = dlt * lax.rsqrt(var + RWKV_GN_EPS) * gg_ref[:, sl] + gb_ref[:, sl]
        rk = r_ref[0, :, sl].astype(F32) * k_ref[0, :, sl].astype(F32)
        bonus = _per_head_sum(rk * rk_ref[:, sl], RWKV_HEAD_DIM)
        out = (yn + bonus * v_ref[0, :, sl].astype(F32)) * g_ref[0, :, sl].astype(F32)
        o_ref[0, :, sl] = out.astype(o_ref.dtype)


def rwkv_post(yf, yb, r, k, v, g, r_k, gn_g, gn_b, tm):
    B, T, C = r.shape
    spec = pl.BlockSpec((1, tm, C), lambda b, i: (b, i, 0))
    vspec = pl.BlockSpec((1, C), lambda b, i: (0, 0))
    return pl.pallas_call(
        _rwkv_post_body,
        grid=(B, T // tm),
        in_specs=[spec] * 6 + [vspec] * 3,
        out_specs=spec,
        out_shape=jax.ShapeDtypeStruct((B, T, C), BF16),
        compiler_params=_cparams(("parallel", "parallel")),
        name="rwkv_post",
    )(yf, yb, r, k, v, g, r_k.reshape(1, C), gn_g.reshape(1, C), gn_b.reshape(1, C))


def _log_sigmoid(x):
    return jnp.minimum(x, 0.0) - jnp.log(1.0 + jnp.exp(-jnp.abs(x)))


def _gla_body(qf, kf, vf, ggf, qb_ref, kb_ref, vb_ref, ggb, gup_ref, gb_ref, of_ref, ob_ref, st_ref,
              *, n_chunks):
    L = CHUNK
    tl = n_chunks * L

    @pl.when(pl.program_id(2) == 0)
    def _():
        st_ref[...] = jnp.zeros_like(st_ref)

    sub = min(GLA_SUB_ROWS, tl)
    ri = lax.broadcasted_iota(jnp.int32, (sub, sub), 0)
    ci = lax.broadcasted_iota(jnp.int32, (sub, sub), 1)
    same_chunk = (ri // L) == (ci // L)
    chunk_rows = [slice(c * L, (c + 1) * L) for c in range(n_chunks)]

    loc = []
    for d, (q_ref, k_ref, v_ref, gg_ref) in enumerate(((qf, kf, vf, ggf), (qb_ref, kb_ref, vb_ref, ggb))):
        lg = _log_sigmoid(_dot(_bf(gg_ref[0]), _bf(gup_ref[d])) + gb_ref[d]) * (1.0 / GLA_GATE_NORMALIZER)
        cum, tot = _decay_factors(lg, reverse=(d == 1))
        k = k_ref[0].astype(F32)
        loc.append(dict(
            v=_bf(v_ref[0]),
            qb=_bf(q_ref[0].astype(F32) * (GLA_DK ** -0.5) * jnp.exp(cum)),
            kb=_bf(k * jnp.exp(-cum)),
            kt=_bf(k * jnp.exp(tot - cum)),
            g_tot=jnp.exp(tot),
            incl=jnp.where(same_chunk, (ri - ci) if d == 0 else (ci - ri), -1) >= 0))
    blocks = [slice(s * sub, (s + 1) * sub) for s in range(tl // sub)]
    atts = [[jnp.where(lc["incl"], _dot_nt(lc["qb"][b], lc["kb"][b]), 0.0) for b in blocks] for lc in loc]
    o_intra = [jnp.concatenate([_dot(_bf(a), lc["v"][b]) for a, b in zip(att, blocks)], axis=0)
               for att, lc in zip(atts, loc)]
    kvs = [[_dot_tn(lc["v"][r], lc["kt"][r]) for r in chunk_rows] for lc in loc]

    st = [st_ref[0], st_ref[1]]
    for j in range(n_chunks):
        for d, o_ref in ((0, of_ref), (1, ob_ref)):
            c = j if d == 0 else n_chunks - 1 - j
            rows = chunk_rows[c]
            o_ref[0, rows, :] = o_intra[d][rows] + _dot_nt(loc[d]["qb"][rows], _bf(st[d]))
            st[d] = st[d] * loc[d]["g_tot"][c * L:c * L + 1] + kvs[d][c]
    st_ref[0] = st[0]
    st_ref[1] = st[1]


def gla_scan(z, gate_up_p, gate_b, tl):
    B, T, _ = z.shape
    H = GLA_HEADS
    nblk = T // tl
    qc = Z_SEGS["q"][2] // GLA_DK
    kc = Z_SEGS["k"][2] // GLA_DK
    vc = Z_SEGS["v"][2] // GLA_DV
    gc = Z_SEGS["gg"][2] // LANES
    specs = []
    for tmap in (lambda i: i, lambda i: nblk - 1 - i):
        specs += [
            pl.BlockSpec((1, tl, GLA_DK), lambda b, h, i, tmap=tmap: (b, tmap(i), qc + h)),
            pl.BlockSpec((1, tl, GLA_DK), lambda b, h, i, tmap=tmap: (b, tmap(i), kc + h)),
            pl.BlockSpec((1, tl, GLA_DV), lambda b, h, i, tmap=tmap: (b, tmap(i), vc + h)),
            pl.BlockSpec((1, tl, LANES), lambda b, h, i, tmap=tmap: (b, tmap(i), gc)),
        ]
    o_sds = jax.ShapeDtypeStruct((B, T, GLA_VDIM), F32)
    return pl.pallas_call(
        functools.partial(_gla_body, n_chunks=tl // CHUNK),
        grid=(B, H, nblk),
        in_specs=specs + [
            pl.BlockSpec((2, LANES, GLA_DK), lambda b, h, i: (0, 0, h)),
            pl.BlockSpec((2, 1, GLA_DK), lambda b, h, i: (0, 0, h)),
        ],
        out_specs=[pl.BlockSpec((1, tl, GLA_DV), lambda b, h, i: (b, i, h)),
                   pl.BlockSpec((1, tl, GLA_DV), lambda b, h, i: (b, nblk - 1 - i, h))],
        out_shape=[o_sds, o_sds],
        scratch_shapes=[pltpu.VMEM((2, GLA_DV, GLA_DK), F32)],
        compiler_params=_cparams(("parallel", "parallel", "arbitrary")),
        name="gla_scan",
    )(z, z, z, z, z, z, z, z, gate_up_p, gate_b.reshape(2, 1, GLA_KDIM))


def _gla_post_body(of_ref, ob_ref, og_ref, g_ref, y_ref):
    o = of_ref[0] + ob_ref[0]
    ms = jnp.mean(o * o, axis=-1, keepdims=True)
    on = o * lax.rsqrt(ms + LN_EPS) * g_ref[...]
    og = og_ref[0].astype(F32)
    y_ref[0] = (on * (og * _sigmoid(og))).astype(y_ref.dtype)


def gla_post(o_f, o_b, z, norm_g, tm):
    B, T, _ = o_f.shape
    H = GLA_HEADS
    oc = Z_SEGS["og"][2] // GLA_DV
    return pl.pallas_call(
        _gla_post_body,
        grid=(B, T // tm, H),
        in_specs=[
            pl.BlockSpec((1, tm, GLA_DV), lambda b, i, h: (b, i, h)),
            pl.BlockSpec((1, tm, GLA_DV), lambda b, i, h: (b, i, h)),
            pl.BlockSpec((1, tm, GLA_DV), lambda b, i, h: (b, i, oc + h)),
            pl.BlockSpec((1, GLA_DV), lambda b, i, h: (0, 0)),
        ],
        out_specs=pl.BlockSpec((1, tm, GLA_DV), lambda b, i, h: (b, i, h)),
        out_shape=jax.ShapeDtypeStruct((B, T, GLA_VDIM), BF16),
        compiler_params=_cparams(("parallel", "parallel", "parallel")),
        name="gla_post",
    )(o_f, o_b, z, norm_g.reshape(1, GLA_DV))


def _pad_ab_columns(w):
    parts, col = [], 0
    for o_start, width, n_start in sorted(Z_SEGS.values(), key=lambda seg: seg[2]):
        if n_start > col:
            parts.append(jnp.zeros((w.shape[0], n_start - col), w.dtype))
        parts.append(w[:, o_start:o_start + width])
        col = n_start + width
    parts.append(jnp.zeros((w.shape[0], Z_COLS - col), w.dtype))
    return jnp.concatenate(parts, axis=1)


def _pad_rows(w, rows):
    pad = [(0, 0)] * w.ndim
    pad[-2] = (0, rows - w.shape[-2])
    return jnp.pad(w, pad)


def _rope_tables(T):
    inv = ROPE_THETA ** (-jnp.arange(0, DIFF_HEAD_DIM, 2, dtype=F32) / DIFF_HEAD_DIM)
    ang = jnp.arange(T, dtype=F32)[:, None] * inv[None, :]
    cos, sin = jnp.cos(ang), jnp.sin(ang)
    zero = jnp.zeros_like(sin)
    cos_t = jnp.tile(cos, (1, 4))
    sin_up = jnp.concatenate([-sin, zero, -sin, zero], axis=1)
    sin_dn = jnp.concatenate([zero, sin, zero, sin], axis=1)
    return cos_t, sin_up, sin_dn


def _lambda_init(layer):
    return 0.8 - 0.6 * math.exp(-0.3 * layer)


def _mixer_ab(x, shift, scale, ab_w_in, rwkv_mu, rwkv_w0, rwkv_w_up, rwkv_a0, rwkv_a_up,
              rwkv_g_up, rwkv_k_k, rwkv_k_a, rwkv_r_k, rwkv_gn_g, rwkv_gn_b,
              gla_gate_up, gla_gate_b, gla_norm_g, blocks):
    w_p = _pad_ab_columns(ab_w_in.astype(BF16))
    z = modulated_projection(x, shift, scale, w_p, BF16, blocks["proj_tm"], 1024)

    mu_p = jnp.zeros((1, Z_A_COLS), F32)
    for name in ("rkv", "wd", "ad", "gd"):
        o_start, width, n_start = Z_SEGS[name]
        mu_p = lax.dynamic_update_slice(mu_p, rwkv_mu[None, o_start:o_start + width], (0, n_start))
    r, k, v, kk, ba, lw, g = rwkv_pre(
        z, mu_p, rwkv_w0, _pad_rows(rwkv_w_up, LORA_PAD), rwkv_a0, _pad_rows(rwkv_a_up, LORA_PAD),
        rwkv_g_up, rwkv_k_k, rwkv_k_a, blocks["pre_tm"])
    y_f, y_r = rwkv_scan(r, k, v, kk, ba, lw, blocks["rwkv_tl"])
    y_a = rwkv_post(y_f, y_r, r, k, v, g, rwkv_r_k, rwkv_gn_g, rwkv_gn_b, blocks["pre_tm"])

    o_f, o_r = gla_scan(z, _pad_rows(gla_gate_up, LANES), gla_gate_b, blocks["scan_tl"])
    y_b = gla_post(o_f, o_r, z, gla_norm_g, blocks["post_tm"])
    return y_a, y_b


def _mixer_c(x, shift, scale, diff_w_in, diff_lambda, diff_subln_g, lambda_init, blocks):
    T = x.shape[1]
    qkv = qkv_projection(x, shift, scale, diff_w_in.astype(BF16), _rope_tables(T),
                         blocks["proj_tm"], 1024)
    return diff_attention(qkv, diff_lambda, diff_subln_g, lambda_init, blocks["attn_tq"])


def _blocks(T):
    return {
        "proj_tm": min(1024, T),
        "pre_tm": min(256, T),
        "post_tm": min(1024, T),
        "rwkv_tl": min(512, T),
        "scan_tl": min(512, T),
        "attn_tq": min(256, T),
        "ln_tm": min(512, T),
        "ffn_tm": min(512, T),
    }


def kernel(x, c, ada_w, ada_b, ln_g, ln_b, ffn_w_in, ffn_w_out, ab_w_in, ab_w_out, rwkv_mu, rwkv_w0, rwkv_w_up, rwkv_a0, rwkv_a_up, rwkv_g_up, rwkv_k_k, rwkv_k_a, rwkv_r_k, rwkv_gn_g, rwkv_gn_b, gla_gate_up, gla_gate_b, gla_norm_g, diff_w_in, diff_w_out, diff_lambda, diff_subln_g):
    blocks = _blocks(x.shape[1])
    shift, scale, gate = ada_modulation(c, ada_w, ada_b)
    ffn_w_in_b = ffn_w_in.astype(BF16)
    ffn_w_out_b = ffn_w_out.astype(BF16)
    for i in range(DEPTH):
        j = i // 2
        m = 2 * i
        if i % 2 == 0:
            y_parts = _mixer_ab(
                x, shift[m], scale[m], ab_w_in[j], rwkv_mu[j], rwkv_w0[j], rwkv_w_up[j],
                rwkv_a0[j], rwkv_a_up[j], rwkv_g_up[j], rwkv_k_k[j], rwkv_k_a[j], rwkv_r_k[j],
                rwkv_gn_g[j], rwkv_gn_b[j], gla_gate_up[j], gla_gate_b[j], gla_norm_g[j], blocks)
            w_out = ab_w_out[j]
        else:
            y_parts = (_mixer_c(x, shift[m], scale[m], diff_w_in[j], diff_lambda[j],
                                diff_subln_g[j], _lambda_init(i), blocks),)
            w_out = diff_w_out[j]
        x = projection_layernorm(y_parts, w_out.astype(BF16), x, gate[m], ln_g[i, 0], ln_b[i, 0],
                                 blocks["ln_tm"])
        x = ffn_sublayer(x, shift[m + 1], scale[m + 1], gate[m + 1], ffn_w_in_b, ffn_w_out_b, i,
                         ln_g[i, 1], ln_b[i, 1], blocks["ffn_tm"], 512)
    return x
```

```python
import functools
import math

import jax
import jax.numpy as jnp
from jax import lax
from jax.experimental import pallas as pl
from jax.experimental.pallas import tpu as pltpu

F32 = jnp.float32
BF16 = jnp.bfloat16
HI = lax.Precision.HIGHEST

D_MODEL = 2048
DEPTH = 2
ALPHA = (2.0 * DEPTH) ** 0.25
LN_EPS = 1e-5

RWKV_HEAD_DIM = 64
RWKV_DIM = D_MODEL // 2
RWKV_HEADS = RWKV_DIM // RWKV_HEAD_DIM
DECAY_LORA = 96
ICL_LORA = 96
GATE_LORA = 256
W_DECAY_SCALE = 0.606531
RWKV_GN_EPS = 64e-5

GLA_HEADS = 4
GLA_VDIM = D_MODEL // 2
GLA_KDIM = GLA_VDIM // 2
GLA_DK = GLA_KDIM // GLA_HEADS
GLA_DV = GLA_VDIM // GLA_HEADS
GLA_GATE_LORA = 16
GLA_GATE_NORMALIZER = 16.0

DIFF_HEAD_DIM = 64
DIFF_V_DIM = 2 * DIFF_HEAD_DIM
DIFF_HEADS = D_MODEL // DIFF_V_DIM
ROPE_THETA = 10000.0

D_FF = -(-(8 * D_MODEL) // (3 * 256)) * 256

LANES = 128
VMEM_LIMIT = 56 * 1024 * 1024

CHUNK = 64
ROPE_SUB_ROWS = 256
FFN_TF = 512
FFN_SUB_ROWS = 256
LN_SUB_ROWS = 128
GLA_SUB_ROWS = 256
LORA_PAD = 128
HALO_ROWS = 16
Z_SEGS = {
    "rkv": (0, 3 * RWKV_DIM, 0),
    "wd": (3 * RWKV_DIM, DECAY_LORA, 3072),
    "ad": (3 * RWKV_DIM + DECAY_LORA, ICL_LORA, 3200),
    "gd": (3 * RWKV_DIM + DECAY_LORA + ICL_LORA, GATE_LORA, 3328),
    "q": (3520, GLA_KDIM, 3584),
    "k": (3520 + GLA_KDIM, GLA_KDIM, 4096),
    "v": (3520 + 2 * GLA_KDIM, GLA_VDIM, 4608),
    "gg": (3520 + 2 * GLA_KDIM + GLA_VDIM, GLA_GATE_LORA, 6656),
    "og": (3520 + 2 * GLA_KDIM + GLA_VDIM + GLA_GATE_LORA, GLA_VDIM, 5632),
}
Z_A_COLS = 3584
Z_COLS = 7168


def _cparams(sem):
    return pltpu.CompilerParams(dimension_semantics=sem, vmem_limit_bytes=VMEM_LIMIT)


def _dot(a, b, prec=None):
    return jnp.dot(a, b, preferred_element_type=F32, precision=prec)


def _dot_nt(a, b, prec=None):
    return lax.dot_general(a, b, (((1,), (1,)), ((), ())), preferred_element_type=F32,
                           precision=prec)


def _dot_tn(a, b, prec=None):
    return lax.dot_general(a, b, (((0,), (0,)), ((), ())), preferred_element_type=F32,
                           precision=prec)


def _sigmoid(x):
    return 1.0 / (1.0 + jnp.exp(-x))


def _layer_norm_rows(u, g, b, eps):
    mu = jnp.mean(u, axis=-1, keepdims=True)
    d = u - mu
    var = jnp.mean(d * d, axis=-1, keepdims=True)
    return d * lax.rsqrt(var + eps) * g + b


def _ada_body(c_ref, w_ref, b_ref, o_ref):
    c = c_ref[...]
    sc = (c * _sigmoid(c)).astype(BF16)
    o_ref[0] = _dot(sc, w_ref[0].astype(BF16)) + b_ref[0]


def ada_modulation(c, ada_w, ada_b):
    B, D = c.shape
    n = ada_w.shape[0] * ada_w.shape[1]
    w = ada_w.reshape(n, D, 3 * D)
    b = ada_b.reshape(n, 1, 3 * D)
    rows = 8
    c_pad = jnp.pad(c, ((0, rows - B), (0, 0)))
    tn = 1024
    out = pl.pallas_call(
        _ada_body,
        grid=(n, 3 * D // tn),
        in_specs=[
            pl.BlockSpec((rows, D), lambda i, j: (0, 0)),
            pl.BlockSpec((1, D, tn), lambda i, j: (i, 0, j)),
            pl.BlockSpec((1, 1, tn), lambda i, j: (i, 0, j)),
        ],
        out_specs=pl.BlockSpec((1, rows, tn), lambda i, j: (i, 0, j)),
        out_shape=jax.ShapeDtypeStruct((n, rows, 3 * D), F32),
        compiler_params=_cparams(("parallel", "parallel")),
        name="ada_modulation",
    )(c_pad, w, b)
    mods = out[:, :B, :]
    shift, scale, gate = mods[..., :D], mods[..., D:2 * D], mods[..., 2 * D:]
    r3 = lambda t: t.reshape(n, B, 1, D)
    return r3(shift), r3(scale), r3(gate)


def _inproj_body(x_ref, sh_ref, sc_ref, w_ref, o_ref, h_ref):
    @pl.when(pl.program_id(2) == 0)
    def _():
        h_ref[...] = (x_ref[0] * (1.0 + sc_ref[0]) + sh_ref[0]).astype(BF16)

    o_ref[0] = _dot(h_ref[...], w_ref[...]).astype(o_ref.dtype)


def modulated_projection(x, shift, scale, w, out_dtype, tm, tn):
    B, T, D = x.shape
    N = w.shape[1]
    return pl.pallas_call(
        _inproj_body,
        grid=(B, T // tm, N // tn),
        in_specs=[
            pl.BlockSpec((1, tm, D), lambda b, i, j: (b, i, 0)),
            pl.BlockSpec((1, 1, D), lambda b, i, j: (b, 0, 0)),
            pl.BlockSpec((1, 1, D), lambda b, i, j: (b, 0, 0)),
            pl.BlockSpec((D, tn), lambda b, i, j: (0, j)),
        ],
        out_specs=pl.BlockSpec((1, tm, tn), lambda b, i, j: (b, i, j)),
        out_shape=jax.ShapeDtypeStruct((B, T, N), out_dtype),
        scratch_shapes=[pltpu.VMEM((tm, D), BF16)],
        compiler_params=_cparams(("parallel", "parallel", "arbitrary")),
        name="modulated_projection",
    )(x, shift, scale, w)


def _qkv_body(x_ref, sh_ref, sc_ref, w_ref, cos_ref, sup_ref, sdn_ref, o_ref, h_ref, *, n_q, n_qk,
              q_scale):
    j = pl.program_id(2)

    @pl.when(j == 0)
    def _():
        h_ref[...] = (x_ref[0] * (1.0 + sc_ref[0]) + sh_ref[0]).astype(BF16)

    is_qk = j < n_qk
    mult = jnp.where(j < n_q, q_scale, 1.0).astype(F32)
    cos = jnp.where(is_qk, cos_ref[...] * mult, 1.0)
    sup = jnp.where(is_qk, sup_ref[...] * mult, 0.0)
    sdn = jnp.where(is_qk, sdn_ref[...] * mult, 0.0)
    half = DIFF_HEAD_DIM // 2
    n_heads, tm = o_ref.shape[1], o_ref.shape[2]
    sub = min(ROPE_SUB_ROWS, tm)
    blocks = [slice(s * sub, (s + 1) * sub) for s in range(tm // sub)]
    accs = [_dot(h_ref[rows, :], w_ref[...]) for rows in blocks]
    for rows, acc in zip(blocks, accs):
        for s in range(n_heads):
            xs = acc[:, s * LANES:(s + 1) * LANES]
            up = pltpu.roll(xs, LANES - half, axis=1)
            dn = pltpu.roll(xs, half, axis=1)
            o_ref[0, s, rows, :] = (
                xs * cos[rows] + up * sup[rows] + dn * sdn[rows]).astype(o_ref.dtype)


def qkv_projection(x, shift, scale, w, rope_tables, tm, tn):
    B, T, D = x.shape
    N = w.shape[1]
    n_q = D_MODEL // tn
    body = functools.partial(_qkv_body, n_q=n_q, n_qk=2 * n_q,
                             q_scale=DIFF_HEAD_DIM ** -0.5 * math.log2(math.e))
    return pl.pallas_call(
        body,
        grid=(B, T // tm, N // tn),
        in_specs=[
            pl.BlockSpec((1, tm, D), lambda b, i, j: (b, i, 0)),
            pl.BlockSpec((1, 1, D), lambda b, i, j: (b, 0, 0)),
            pl.BlockSpec((1, 1, D), lambda b, i, j: (b, 0, 0)),
            pl.BlockSpec((D, tn), lambda b, i, j: (0, j)),
            pl.BlockSpec((tm, LANES), lambda b, i, j: (i, 0)),
            pl.BlockSpec((tm, LANES), lambda b, i, j: (i, 0)),
            pl.BlockSpec((tm, LANES), lambda b, i, j: (i, 0)),
        ],
        out_specs=pl.BlockSpec((1, tn // LANES, tm, LANES), lambda b, i, j: (b, j, i, 0)),
        out_shape=jax.ShapeDtypeStruct((B, N // LANES, T, LANES), BF16),
        scratch_shapes=[pltpu.VMEM((tm, D), BF16)],
        compiler_params=_cparams(("parallel", "parallel", "arbitrary")),
        name="qkv_projection",
    )(x, shift, scale, w, *rope_tables)


def _proj_ln_body(*refs, n_parts):
    y_refs = refs[:n_parts]
    w_ref, x_ref, gate_ref, g_ref, b_ref, o_ref = refs[n_parts:]
    tm = x_ref.shape[1]
    sub = min(LN_SUB_ROWS, tm)
    accs = []
    for s in range(tm // sub):
        rows = slice(s * sub, (s + 1) * sub)
        acc = None
        off = 0
        for yr in y_refs:
            k = yr.shape[-1]
            part = _dot(yr[0, rows, :], w_ref[off:off + k, :])
            acc = part if acc is None else acc + part
            off += k
        accs.append(acc)
    for s, acc in enumerate(accs):
        rows = slice(s * sub, (s + 1) * sub)
        u = ALPHA * x_ref[0, rows, :] + (1.0 + gate_ref[0]) * acc
        o_ref[0, rows, :] = _layer_norm_rows(u, g_ref[...], b_ref[...], LN_EPS)


def projection_layernorm(y_parts, w, x, gate, ln_g, ln_b, tm):
    B, T, D = x.shape
    K = w.shape[0]
    in_specs = [pl.BlockSpec((1, tm, yp.shape[-1]), lambda b, i: (b, i, 0)) for yp in y_parts]
    in_specs += [
        pl.BlockSpec((K, D), lambda b, i: (0, 0)),
        pl.BlockSpec((1, tm, D), lambda b, i: (b, i, 0)),
        pl.BlockSpec((1, 1, D), lambda b, i: (b, 0, 0)),
        pl.BlockSpec((1, D), lambda b, i: (0, 0)),
        pl.BlockSpec((1, D), lambda b, i: (0, 0)),
    ]
    return pl.pallas_call(
        functools.partial(_proj_ln_body, n_parts=len(y_parts)),
        grid=(B, T // tm),
        in_specs=in_specs,
        out_specs=pl.BlockSpec((1, tm, D), lambda b, i: (b, i, 0)),
        out_shape=jax.ShapeDtypeStruct((B, T, D), F32),
        compiler_params=_cparams(("parallel", "parallel")),
        name="projection_layernorm",
    )(*y_parts, w, x, gate, ln_g.reshape(1, D), ln_b.reshape(1, D))


def _ffn_body(x_ref, sh_ref, sc_ref, gate_ref, wg_ref, wu_ref, wo_ref, g_ref, b_ref, o_ref,
              h_ref, acc_ref):
    j = pl.program_id(2)
    last = pl.num_programs(2) - 1
    tm = x_ref.shape[1]
    sub = min(FFN_SUB_ROWS, tm)
    blocks = [slice(s * sub, (s + 1) * sub) for s in range(tm // sub)]

    def contribution(h):
        gt = _dot(h, wg_ref[0, 0])
        up = _dot(h, wu_ref[0, 0])
        act = (gt * _sigmoid(gt) * up).astype(BF16)
        return _dot(act, wo_ref[0])

    @pl.when(j == 0)
    def _():
        for rows in blocks:
            h = (x_ref[0, rows, :] * (1.0 + sc_ref[0]) + sh_ref[0]).astype(BF16)
            h_ref[rows, :] = h
            acc_ref[rows, :] = contribution(h)

    @pl.when((j > 0) & (j < last))
    def _():
        acc_ref[...] += contribution(h_ref[...])

    @pl.when(j == last)
    def _():
        for rows in blocks:
            acc = acc_ref[rows, :] + contribution(h_ref[rows, :])
            u = ALPHA * x_ref[0, rows, :] + (1.0 + gate_ref[0]) * acc
            o_ref[0, rows, :] = _layer_norm_rows(u, g_ref[...], b_ref[...], LN_EPS)


def ffn_sublayer(x, shift, scale, gate, w_in, w_out, layer, ln_g, ln_b, tm):
    B, T, D = x.shape
    F = w_out.shape[1]
    tf = w_in.shape[-1]
    nf = F // tf
    assert nf >= 2, "the first and last F-steps must be distinct grid steps"
    return pl.pallas_call(
        _ffn_body,
        grid=(B, T // tm, nf),
        in_specs=[
            pl.BlockSpec((1, tm, D), lambda b, i, j: (b, i, 0)),
            pl.BlockSpec((1, 1, D), lambda b, i, j: (b, 0, 0)),
            pl.BlockSpec((1, 1, D), lambda b, i, j: (b, 0, 0)),
            pl.BlockSpec((1, 1, D), lambda b, i, j: (b, 0, 0)),
            pl.BlockSpec((1, 1, D, tf), lambda b, i, j: (layer, j, 0, 0)),
            pl.BlockSpec((1, 1, D, tf), lambda b, i, j: (layer, j + nf, 0, 0)),
            pl.BlockSpec((1, tf, D), lambda b, i, j: (layer, j, 0)),
            pl.BlockSpec((1, D), lambda b, i, j: (0, 0)),
            pl.BlockSpec((1, D), lambda b, i, j: (0, 0)),
        ],
        out_specs=pl.BlockSpec((1, tm, D), lambda b, i, j: (b, i, 0)),
        out_shape=jax.ShapeDtypeStruct((B, T, D), F32),
        scratch_shapes=[pltpu.VMEM((tm, D), BF16), pltpu.VMEM((tm, D), F32)],
        compiler_params=_cparams(("parallel", "parallel", "arbitrary")),
        name="ffn_sublayer",
    )(x, shift, scale, gate, w_in, w_in, w_out, ln_g.reshape(1, D), ln_b.reshape(1, D))


ONES_ROWS = 16


KEY_CHUNK = 512


def _attn_body(q0_ref, qn_ref, kn_ref, v_ref, lam_ref, g_ref, o_ref, vt_ref, s_ref, m_ref,
               *, lambda_init):
    dv = DIFF_V_DIM
    T = kn_ref.shape[2]
    first_tile = pl.program_id(2) == 0
    first_step = first_tile & (pl.program_id(0) == 0) & (pl.program_id(1) == 0)
    lane = lax.broadcasted_iota(jnp.int32, (1, LANES), 1)
    comp0 = lane < DIFF_HEAD_DIM

    def comps(q):
        zero = jnp.zeros_like(q)
        return (jnp.where(comp0, q, zero), jnp.where(comp0, zero, q))

    def chunk_scores(qcs, rows):
        kc = kn_ref[0, 0, rows, :]
        return [_dot_nt(kc, qc) for qc in qcs]

    def col_max(s):
        return jnp.max(s, axis=0, keepdims=True)

    chunks = [slice(c * KEY_CHUNK, (c + 1) * KEY_CHUNK) for c in range(T // KEY_CHUNK)]

    @pl.when(first_tile)
    def _():
        vt_ref[0:dv, :] = v_ref[0, 0].astype(F32).T.astype(BF16)
        vt_ref[dv:, :] = jnp.ones((ONES_ROWS, T), BF16)

    @pl.when(first_step)
    def _():
        qcs = comps(q0_ref[0, 0])
        m = [None, None]
        for rows in chunks:
            s = chunk_scores(qcs, rows)
            for j in range(2):
                s_ref[j, rows, :] = s[j]
                m[j] = col_max(s[j]) if m[j] is None else jnp.maximum(m[j], col_max(s[j]))
        for j in range(2):
            m_ref[j] = m[j]

    qcs = comps(qn_ref[0, 0])
    m_cur = [m_ref[0], m_ref[1]]
    m_new = [None, None]
    acc = [None, None]
    for rows in chunks:
        es = [jnp.exp2(s_ref[j, rows, :] - m_cur[j]).astype(BF16) for j in range(2)]
        for j in range(2):
            part = _dot(vt_ref[:, rows], es[j])
            acc[j] = part if acc[j] is None else acc[j] + part
        s_new = chunk_scores(qcs, rows)
        for j in range(2):
            s_ref[j, rows, :] = s_new[j]
            m_new[j] = col_max(s_new[j]) if m_new[j] is None else jnp.maximum(m_new[j], col_max(s_new[j]))
    for j in range(2):
        m_ref[j] = m_new[j]

    lp = lam_ref[...]
    lam = (jnp.exp(jnp.sum(lp[0:1] * lp[1:2], axis=-1, keepdims=True))
           - jnp.exp(jnp.sum(lp[2:3] * lp[3:4], axis=-1, keepdims=True)) + lambda_init)
    outs = [a[0:dv] / a[dv:dv + 1] for a in acc]
    o = (outs[0] - lam * outs[1]).T
    ms = jnp.mean(o * o, axis=-1, keepdims=True)
    o = o * lax.rsqrt(ms + LN_EPS) * g_ref[...] * (1.0 - lambda_init)
    o_ref[0] = o.astype(o_ref.dtype)


def diff_attention(qkv, lam_params, subln_g, lambda_init, tq):
    B, _, T, _ = qkv.shape
    H = DIFF_HEADS
    nq = T // tq

    def nxt(b, h, i):
        lin = jnp.minimum((b * H + h) * nq + i + 1, B * H * nq - 1)
        return lin // (H * nq), (lin // nq) % H, lin % nq

    def qn_map(b, h, i):
        nb, nh, nt = nxt(b, h, i)
        return nb, nh, nt, 0

    def kn_map(b, h, i):
        nb, nh, _ = nxt(b, h, i)
        return nb, H + nh, 0, 0

    return pl.pallas_call(
        functools.partial(_attn_body, lambda_init=lambda_init),
        grid=(B, H, nq),
        in_specs=[
            pl.BlockSpec((1, 1, tq, LANES), lambda b, h, i: (0, 0, 0, 0)),
            pl.BlockSpec((1, 1, tq, LANES), qn_map),
            pl.BlockSpec((1, 1, T, LANES), kn_map),
            pl.BlockSpec((1, 1, T, LANES), lambda b, h, i: (b, 2 * H + h, 0, 0)),
            pl.BlockSpec((4, DIFF_HEAD_DIM), lambda b, h, i: (0, 0)),
            pl.BlockSpec((1, LANES), lambda b, h, i: (0, 0)),
        ],
        out_specs=pl.BlockSpec((1, tq, LANES), lambda b, h, i: (b, i, h)),
        out_shape=jax.ShapeDtypeStruct((B, T, H * DIFF_V_DIM), BF16),
        scratch_shapes=[pltpu.VMEM((DIFF_V_DIM + ONES_ROWS, T), BF16),
                        pltpu.VMEM((2, T, tq), F32),
                        pltpu.VMEM((2, 1, tq), F32)],
        compiler_params=_cparams(("arbitrary", "arbitrary", "arbitrary")),
        name="diff_attention",
    )(qkv, qkv, qkv, qkv, lam_params, subln_g.reshape(1, LANES))


def _per_head_sum(x, head_dim):
    lane = lax.broadcasted_iota(jnp.int32, (1, LANES), 1)
    lo = lane < head_dim
    s_lo = jnp.sum(jnp.where(lo, x, 0.0), axis=-1, keepdims=True)
    s_hi = jnp.sum(jnp.where(lo, 0.0, x), axis=-1, keepdims=True)
    return jnp.where(lo, s_lo, s_hi)


def _rwkv_pre_body(z_ref, zp_ref, zn_ref, mu_ref, w0_ref, wup_ref, a0_ref, aup_ref, gup_ref,
                   kk_ref, ka_ref, r_o, k_o, v_o, kk_o, ba_o, lw_o, g_o, *, tm):
    i = pl.program_id(1)
    last = pl.num_programs(1) - 1
    row8 = lax.broadcasted_iota(jnp.int32, (8, 1), 0)
    C = RWKV_DIM

    def shifted(lo, hi):
        z = z_ref[0, :, lo:hi].astype(F32)
        prev = jnp.where(i == 0, 0.0, zp_ref[0, HALO_ROWS - 1:HALO_ROWS, lo:hi].astype(F32))
        nxt = jnp.where(i == last, 0.0, zn_ref[0, 0:1, lo:hi].astype(F32))
        nb = pltpu.roll(z, 1, axis=0) + pltpu.roll(z, tm - 1, axis=0)
        top = nb[0:8] + jnp.where(row8 == 0, prev - z[tm - 1:tm], 0.0)
        bot = nb[tm - 8:tm] + jnp.where(row8 == 7, nxt - z[0:1], 0.0)
        nb = jnp.concatenate([top, nb[8:tm - 8], bot], axis=0)
        mu = mu_ref[:, lo:hi]
        return z * (1.0 - mu) + nb * (0.5 * mu)

    r = shifted(0, C)
    k = shifted(C, 2 * C)
    v = shifted(2 * C, 3 * C)
    wd = shifted(3 * C, 3 * C + LORA_PAD)
    ad = shifted(3 * C + LORA_PAD, 3 * C + 2 * LORA_PAD)
    gd = shifted(3 * C + 2 * LORA_PAD, 3 * C + 2 * LORA_PAD + GATE_LORA)

    r_o[0] = r.astype(r_o.dtype)
    v_o[0] = v.astype(v_o.dtype)
    twd = jnp.tanh(wd).astype(BF16)
    for d in range(2):
        lw_o[d, 0] = -W_DECAY_SCALE * _sigmoid(w0_ref[d] + _dot(twd, wup_ref[d].astype(BF16)))
    a = _sigmoid(a0_ref[...] + _dot(ad.astype(BF16), aup_ref[...].astype(BF16)))
    g_o[0] = _dot(_sigmoid(gd).astype(BF16), gup_ref[...].astype(BF16)).astype(g_o.dtype)
    k_o[0] = (k * (1.0 + (a - 1.0) * ka_ref[...])).astype(k_o.dtype)
    kk0 = k * kk_ref[...]
    for s in range(C // LANES):
        sl = slice(s * LANES, (s + 1) * LANES)
        x = kk0[:, sl]
        nrm = jnp.maximum(jnp.sqrt(_per_head_sum(x * x, RWKV_HEAD_DIM)), 1e-12)
        kk = x / nrm
        kk_o[0, :, sl] = kk.astype(kk_o.dtype)
        ba_o[0, :, sl] = (kk * a[:, sl]).astype(ba_o.dtype)


def rwkv_pre(z, mu_p, w0, wup_p, a0, aup_p, gup, k_k, k_a, tm):
    B, T, _ = z.shape
    C = RWKV_DIM
    nbh = tm // HALO_ROWS
    nh = T // HALO_ROWS
    bt = jax.ShapeDtypeStruct((B, T, C), BF16)
    vec = lambda t: t.reshape(1, C)
    full = lambda shp: pl.BlockSpec(shp, lambda b, i: (0,) * len(shp))
    body = functools.partial(_rwkv_pre_body, tm=tm)
    return pl.pallas_call(
        body,
        grid=(B, T // tm),
        in_specs=[
            pl.BlockSpec((1, tm, Z_A_COLS), lambda b, i: (b, i, 0)),
            pl.BlockSpec((1, HALO_ROWS, Z_A_COLS), lambda b, i: (b, jnp.maximum(i * nbh - 1, 0), 0)),
            pl.BlockSpec((1, HALO_ROWS, Z_A_COLS),
                         lambda b, i: (b, jnp.minimum((i + 1) * nbh, nh - 1), 0)),
            full((1, Z_A_COLS)),
            full((2, 1, C)),
            full((2, LORA_PAD, C)),
            full((1, C)),
            full((LORA_PAD, C)),
            full((GATE_LORA, C)),
            full((1, C)),
            full((1, C)),
        ],
        out_specs=[pl.BlockSpec((1, tm, C), lambda b, i: (b, i, 0))] * 5
        + [pl.BlockSpec((2, 1, tm, C), lambda b, i: (0, b, i, 0)),
           pl.BlockSpec((1, tm, C), lambda b, i: (b, i, 0))],
        out_shape=[bt] * 5 + [jax.ShapeDtypeStruct((2, B, T, C), F32), bt],
        compiler_params=_cparams(("parallel", "parallel")),
        name="rwkv_pre",
    )(z, z, z, mu_p, w0.reshape(2, 1, C), wup_p, vec(a0), aup_p, gup, vec(k_k), vec(k_a))


def _bf(x):
    return x.astype(BF16)


def _segmented_cumsum(x, seg):
    pos = lax.broadcasted_iota(jnp.int32, (x.shape[0], 1), 0) % seg
    s = 1
    while s < seg:
        x = x + jnp.where(pos >= s, pltpu.roll(x, s, axis=0), 0.0)
        s *= 2
    return x


def _segment_totals(x, seg):
    parts = []
    for c in range(x.shape[0] // seg):
        t = jnp.sum(x[c * seg:(c + 1) * seg], axis=0, keepdims=True)
        parts.append(jnp.broadcast_to(t, (seg, x.shape[1])))
    return jnp.concatenate(parts, axis=0)


def _decay_factors(lw, reverse):
    tot = _segment_totals(lw, CHUNK)
    cum = _segmented_cumsum(lw, CHUNK)
    if reverse:
        cum = tot - cum + lw
    return cum, tot


def _rwkv_chunks_local(chunks):
    L = CHUNK
    lane = lax.broadcasted_iota(jnp.int32, (1, LANES), 1)
    h0 = lane < RWKV_HEAD_DIM

    def stack(x):
        return jnp.concatenate([jnp.where(h0, x, 0.0), jnp.where(h0, 0.0, x)], axis=0)

    pre = []
    for r, k, v, kk, ba, lw, cum, tot, strict, incl in chunks:
        g_ex = jnp.exp(cum - lw)
        g_inv = jnp.exp(-cum)
        g_rem = jnp.exp(tot - cum)
        a_st = stack(-kk * g_ex)
        r_st = stack(r * jnp.exp(cum))
        lhs = _bf(jnp.concatenate([a_st, r_st], axis=0))
        rhs = _bf(jnp.concatenate([stack(ba * g_inv), stack(k * g_inv)], axis=0))
        t_b = _bf(jnp.concatenate([stack(ba * g_rem), stack(k * g_rem)], axis=0))
        pre.append((a_st, r_st, lhs, rhs, t_b, _bf(stack(v))))

    ps = [_dot_nt(lhs, rhs) for _, _, lhs, rhs, _, _ in pre]
    m_ab, m_ak, m_r = [], [], []
    for p, ch in zip(ps, chunks):
        strict, incl = ch[8], ch[9]
        m_ab.append(jnp.where(strict, p[:2 * L, :2 * L], 0.0))
        m_ak.append(_bf(jnp.where(strict, p[:2 * L, 2 * L:], 0.0)))
        m_r.append(_bf(jnp.concatenate([jnp.where(incl, p[2 * L:, :2 * L], 0.0),
                                        jnp.where(incl, p[2 * L:, 2 * L:], 0.0)], axis=1)))

    akv = [_dot(m, pr[5]) for m, pr in zip(m_ak, pre)]
    xs = [jnp.concatenate([pr[0], t], axis=1) for pr, t in zip(pre, akv)]
    ns = m_ab
    steps = int(math.log2(L))
    for it in range(steps):
        n_bs = [_bf(n) for n in ns]
        xs = [x + _dot(n_b, _bf(x)) for x, n_b in zip(xs, n_bs)]
        if it + 1 < steps:
            ns = [_dot(n_b, n_b) for n_b in n_bs]

    out = []
    wu = [(_bf(x[:, :LANES]), _bf(x[:, LANES:])) for x in xs]
    qys = [_dot(m, jnp.concatenate([jnp.concatenate([w_b, ul_b], axis=1),
                                    jnp.concatenate([jnp.zeros_like(pr[5]), pr[5]], axis=1)], axis=0))
           for m, (w_b, ul_b), pr in zip(m_r, wu, pre)]
    gps = [_dot_tn(w_b, pr[4][:2 * L]) for (w_b, _), pr in zip(wu, pre)]
    hs = [_dot_tn(jnp.concatenate([ul_b, pr[5]], axis=0), pr[4])
          for (_, ul_b), pr in zip(wu, pre)]
    for qy, gp, h, pr in zip(qys, gps, hs, pre):
        out.append((_bf(pr[1] + qy[:, :LANES]), qy[:, LANES:], _bf(gp), h))
    return out


def _rwkv_scan_body(rf, kf, vf, kkf, baf, lwf, rb, kb, vb, kkb, bab, lwb, yf_ref, yb_ref, s_ref,
                    *, n_chunks):
    L = CHUNK

    @pl.when(pl.program_id(2) == 0)
    def _():
        s_ref[...] = jnp.zeros_like(s_ref)

    r2 = lax.broadcasted_iota(jnp.int32, (2 * L, 2 * L), 0)
    c2 = lax.broadcasted_iota(jnp.int32, (2 * L, 2 * L), 1)
    same_head = (r2 // L) == (c2 // L)
    dt = r2 % L - c2 % L

    chunks, g_tots = [], []
    for d, refs in ((0, (rf, kf, vf, kkf, baf, lwf)), (1, (rb, kb, vb, kkb, bab, lwb))):
        r_ref, k_ref, v_ref, kk_ref, ba_ref, lw_ref = refs
        strict = jnp.where(same_head, dt if d == 0 else -dt, -1) > 0
        incl = strict | (r2 == c2)
        lw_all = lw_ref[0, 0]
        cum_all, tot_all = _decay_factors(lw_all, reverse=(d == 1))
        g_tot_all = jnp.exp(tot_all)
        for c in range(n_chunks):
            rows = slice(c * L, (c + 1) * L)
            chunks.append((r_ref[0, rows, :], k_ref[0, rows, :], v_ref[0, rows, :],
                           kk_ref[0, rows, :], ba_ref[0, rows, :], lw_all[rows], cum_all[rows],
                           tot_all[rows], strict, incl))
            g_tots.append(g_tot_all[c * L:c * L + 1])
    local = _rwkv_chunks_local(chunks)

    s2 = [s_ref[0], s_ref[1]]
    for j in range(n_chunks):
        for d, y_ref in ((0, yf_ref), (1, yb_ref)):
            c = j if d == 0 else n_chunks - 1 - j
            q_b, yl_st, gp_b, h = local[d * n_chunks + c]
            s_b = _bf(s2[d])
            y_st = _dot_nt(q_b, s_b) + yl_st
            y_ref[0, c * L:(c + 1) * L, :] = y_st[:L] + y_st[L:]
            s2[d] = s2[d] * g_tots[d * n_chunks + c] + _dot(s_b, gp_b) + h
    s_ref[0] = s2[0]
    s_ref[1] = s2[1]


def rwkv_scan(r, k, v, kk, ba, lw, tl):
    B, T, C = r.shape
    n_pairs = C // LANES
    nblk = T // tl
    fwd = pl.BlockSpec((1, tl, LANES), lambda b, p, i: (b, i, p))
    bwd = pl.BlockSpec((1, tl, LANES), lambda b, p, i: (b, nblk - 1 - i, p))
    lw_f = pl.BlockSpec((1, 1, tl, LANES), lambda b, p, i: (0, b, i, p))
    lw_b = pl.BlockSpec((1, 1, tl, LANES), lambda b, p, i: (1, b, nblk - 1 - i, p))
    y_sds = jax.ShapeDtypeStruct((B, T, C), F32)
    return pl.pallas_call(
        functools.partial(_rwkv_scan_body, n_chunks=tl // CHUNK),
        grid=(B, n_pairs, nblk),
        in_specs=[fwd] * 5 + [lw_f] + [bwd] * 5 + [lw_b],
        out_specs=[fwd, bwd],
        out_shape=[y_sds, y_sds],
        scratch_shapes=[pltpu.VMEM((2, LANES, LANES), F32)],
        compiler_params=_cparams(("parallel", "parallel", "arbitrary")),
        name="rwkv_scan",
    )(r, k, v, kk, ba, lw, r, k, v, kk, ba, lw)


def _rwkv_post_body(yf_ref, yb_ref, r_ref, k_ref, v_ref, g_ref, rk_ref, gg_ref, gb_ref, o_ref):
    C = RWKV_DIM
    for s in range(C // LANES):
        sl = slice(s * LANES, (s + 1) * LANES)
        y = yf_ref[0, :, sl] + yb_ref[0, :, sl]
        inv_n = 1.0 / RWKV_HEAD_DIM
        mu = _per_head_sum(y, RWKV_HEAD_DIM) * inv_n
        dlt = y - mu
        var = _per_head_sum(dlt * dlt, RWKV_HEAD_DIM) * inv_n
        yn = dlt * lax.rsqrt(var + RWKV_GN_EPS) * gg_ref[:, sl] + gb_ref[:, sl]
        rk = r_ref[0, :, sl].astype(F32) * k_ref[0, :, sl].astype(F32)
        bonus = _per_head_sum(rk * rk_ref[:, sl], RWKV_HEAD_DIM)
        out = (yn + bonus * v_ref[0, :, sl].astype(F32)) * g_ref[0, :, sl].astype(F32)
        o_ref[0, :, sl] = out.astype(o_ref.dtype)


def rwkv_post(yf, yb, r, k, v, g, r_k, gn_g, gn_b, tm):
    B, T, C = r.shape
    spec = pl.BlockSpec((1, tm, C), lambda b, i: (b, i, 0))
    vspec = pl.BlockSpec((1, C), lambda b, i: (0, 0))
    return pl.pallas_call(
        _rwkv_post_body,
        grid=(B, T // tm),
        in_specs=[spec] * 6 + [vspec] * 3,
        out_specs=spec,
        out_shape=jax.ShapeDtypeStruct((B, T, C), BF16),
        compiler_params=_cparams(("parallel", "parallel")),
        name="rwkv_post",
    )(yf, yb, r, k, v, g, r_k.reshape(1, C), gn_g.reshape(1, C), gn_b.reshape(1, C))


def _log_sigmoid(x):
    return jnp.minimum(x, 0.0) - jnp.log(1.0 + jnp.exp(-jnp.abs(x)))


def _gla_body(qf, kf, vf, ggf, qb_ref, kb_ref, vb_ref, ggb, gup_ref, gb_ref, of_ref, ob_ref, st_ref,
              *, n_chunks):
    L = CHUNK
    tl = n_chunks * L

    @pl.when(pl.program_id(2) == 0)
    def _():
        st_ref[...] = jnp.zeros_like(st_ref)

    sub = min(GLA_SUB_ROWS, tl)
    ri = lax.broadcasted_iota(jnp.int32, (sub, sub), 0)
    ci = lax.broadcasted_iota(jnp.int32, (sub, sub), 1)
    same_chunk = (ri // L) == (ci // L)
    chunk_rows = [slice(c * L, (c + 1) * L) for c in range(n_chunks)]

    loc = []
    for d, (q_ref, k_ref, v_ref, gg_ref) in enumerate(((qf, kf, vf, ggf), (qb_ref, kb_ref, vb_ref, ggb))):
        lg = _log_sigmoid(_dot(_bf(gg_ref[0]), _bf(gup_ref[d])) + gb_ref[d]) * (1.0 / GLA_GATE_NORMALIZER)
        cum, tot = _decay_factors(lg, reverse=(d == 1))
        k = k_ref[0].astype(F32)
        loc.append(dict(
            v=_bf(v_ref[0]),
            qb=_bf(q_ref[0].astype(F32) * (GLA_DK ** -0.5) * jnp.exp(cum)),
            kb=_bf(k * jnp.exp(-cum)),
            kt=_bf(k * jnp.exp(tot - cum)),
            g_tot=jnp.exp(tot),
            incl=jnp.where(same_chunk, (ri - ci) if d == 0 else (ci - ri), -1) >= 0))
    blocks = [slice(s * sub, (s + 1) * sub) for s in range(tl // sub)]
    atts = [[jnp.where(lc["incl"], _dot_nt(lc["qb"][b], lc["kb"][b]), 0.0) for b in blocks] for lc in loc]
    o_intra = [jnp.concatenate([_dot(_bf(a), lc["v"][b]) for a, b in zip(att, blocks)], axis=0)
               for att, lc in zip(atts, loc)]
    kvs = [[_dot_tn(lc["v"][r], lc["kt"][r]) for r in chunk_rows] for lc in loc]

    st = [st_ref[0], st_ref[1]]
    for j in range(n_chunks):
        for d, o_ref in ((0, of_ref), (1, ob_ref)):
            c = j if d == 0 else n_chunks - 1 - j
            rows = chunk_rows[c]
            o_ref[0, rows, :] = o_intra[d][rows] + _dot_nt(loc[d]["qb"][rows], _bf(st[d]))
            st[d] = st[d] * loc[d]["g_tot"][c * L:c * L + 1] + kvs[d][c]
    st_ref[0] = st[0]
    st_ref[1] = st[1]


def gla_scan(z, gate_up_p, gate_b, tl):
    B, T, _ = z.shape
    H = GLA_HEADS
    nblk = T // tl
    qc = Z_SEGS["q"][2] // GLA_DK
    kc = Z_SEGS["k"][2] // GLA_DK
    vc = Z_SEGS["v"][2] // GLA_DV
    gc = Z_SEGS["gg"][2] // LANES
    specs = []
    for tmap in (lambda i: i, lambda i: nblk - 1 - i):
        specs += [
            pl.BlockSpec((1, tl, GLA_DK), lambda b, h, i, tmap=tmap: (b, tmap(i), qc + h)),
            pl.BlockSpec((1, tl, GLA_DK), lambda b, h, i, tmap=tmap: (b, tmap(i), kc + h)),
            pl.BlockSpec((1, tl, GLA_DV), lambda b, h, i, tmap=tmap: (b, tmap(i), vc + h)),
            pl.BlockSpec((1, tl, LANES), lambda b, h, i, tmap=tmap: (b, tmap(i), gc)),
        ]
    o_sds = jax.ShapeDtypeStruct((B, T, GLA_VDIM), F32)
    return pl.pallas_call(
        functools.partial(_gla_body, n_chunks=tl // CHUNK),
        grid=(B, H, nblk),
        in_specs=specs + [
            pl.BlockSpec((2, LANES, GLA_DK), lambda b, h, i: (0, 0, h)),
            pl.BlockSpec((2, 1, GLA_DK), lambda b, h, i: (0, 0, h)),
        ],
        out_specs=[pl.BlockSpec((1, tl, GLA_DV), lambda b, h, i: (b, i, h)),
                   pl.BlockSpec((1, tl, GLA_DV), lambda b, h, i: (b, nblk - 1 - i, h))],
        out_shape=[o_sds, o_sds],
        scratch_shapes=[pltpu.VMEM((2, GLA_DV, GLA_DK), F32)],
        compiler_params=_cparams(("parallel", "parallel", "arbitrary")),
        name="gla_scan",
    )(z, z, z, z, z, z, z, z, gate_up_p, gate_b.reshape(2, 1, GLA_KDIM))


def _gla_post_body(of_ref, ob_ref, og_ref, g_ref, y_ref):
    o = of_ref[0] + ob_ref[0]
    ms = jnp.mean(o * o, axis=-1, keepdims=True)
    on = o * lax.rsqrt(ms + LN_EPS) * g_ref[...]
    og = og_ref[0].astype(F32)
    y_ref[0] = (on * (og * _sigmoid(og))).astype(y_ref.dtype)


def gla_post(o_f, o_b, z, norm_g, tm):
    B, T, _ = o_f.shape
    H = GLA_HEADS
    oc = Z_SEGS["og"][2] // GLA_DV
    return pl.pallas_call(
        _gla_post_body,
        grid=(B, T // tm, H),
        in_specs=[
            pl.BlockSpec((1, tm, GLA_DV), lambda b, i, h: (b, i, h)),
            pl.BlockSpec((1, tm, GLA_DV), lambda b, i, h: (b, i, h)),
            pl.BlockSpec((1, tm, GLA_DV), lambda b, i, h: (b, i, oc + h)),
            pl.BlockSpec((1, GLA_DV), lambda b, i, h: (0, 0)),
        ],
        out_specs=pl.BlockSpec((1, tm, GLA_DV), lambda b, i, h: (b, i, h)),
        out_shape=jax.ShapeDtypeStruct((B, T, GLA_VDIM), BF16),
        compiler_params=_cparams(("parallel", "parallel", "parallel")),
        name="gla_post",
    )(o_f, o_b, z, norm_g.reshape(1, GLA_DV))


def _pad_ab_columns(w):
    parts, col = [], 0
    for o_start, width, n_start in sorted(Z_SEGS.values(), key=lambda seg: seg[2]):
        if n_start > col:
            parts.append(jnp.zeros((w.shape[0], n_start - col), w.dtype))
        parts.append(w[:, o_start:o_start + width])
        col = n_start + width
    parts.append(jnp.zeros((w.shape[0], Z_COLS - col), w.dtype))
    return jnp.concatenate(parts, axis=1)


def _pad_rows(w, rows):
    pad = [(0, 0)] * w.ndim
    pad[-2] = (0, rows - w.shape[-2])
    return jnp.pad(w, pad)


def _rope_tables(T):
    inv = ROPE_THETA ** (-jnp.arange(0, DIFF_HEAD_DIM, 2, dtype=F32) / DIFF_HEAD_DIM)
    ang = jnp.arange(T, dtype=F32)[:, None] * inv[None, :]
    cos, sin = jnp.cos(ang), jnp.sin(ang)
    zero = jnp.zeros_like(sin)
    cos_t = jnp.tile(cos, (1, 4))
    sin_up = jnp.concatenate([-sin, zero, -sin, zero], axis=1)
    sin_dn = jnp.concatenate([zero, sin, zero, sin], axis=1)
    return cos_t, sin_up, sin_dn


def _lambda_init(layer):
    return 0.8 - 0.6 * math.exp(-0.3 * layer)


def _mixer_ab(x, shift, scale, ab_w_in, rwkv_mu, rwkv_w0, rwkv_w_up, rwkv_a0, rwkv_a_up,
              rwkv_g_up, rwkv_k_k, rwkv_k_a, rwkv_r_k, rwkv_gn_g, rwkv_gn_b,
              gla_gate_up, gla_gate_b, gla_norm_g, blocks):
    w_p = _pad_ab_columns(ab_w_in.astype(BF16))
    z = modulated_projection(x, shift, scale, w_p, BF16, blocks["proj_tm"], 1024)

    mu_p = jnp.zeros((1, Z_A_COLS), F32)
    for name in ("rkv", "wd", "ad", "gd"):
        o_start, width, n_start = Z_SEGS[name]
        mu_p = lax.dynamic_update_slice(mu_p, rwkv_mu[None, o_start:o_start + width], (0, n_start))
    r, k, v, kk, ba, lw, g = rwkv_pre(
        z, mu_p, rwkv_w0, _pad_rows(rwkv_w_up, LORA_PAD), rwkv_a0, _pad_rows(rwkv_a_up, LORA_PAD),
        rwkv_g_up, rwkv_k_k, rwkv_k_a, blocks["pre_tm"])
    y_f, y_r = rwkv_scan(r, k, v, kk, ba, lw, blocks["rwkv_tl"])
    y_a = rwkv_post(y_f, y_r, r, k, v, g, rwkv_r_k, rwkv_gn_g, rwkv_gn_b, blocks["pre_tm"])

    o_f, o_r = gla_scan(z, _pad_rows(gla_gate_up, LANES), gla_gate_b, blocks["scan_tl"])
    y_b = gla_post(o_f, o_r, z, gla_norm_g, blocks["post_tm"])
    return y_a, y_b


def _mixer_c(x, shift, scale, diff_w_in, diff_lambda, diff_subln_g, lambda_init, blocks):
    T = x.shape[1]
    qkv = qkv_projection(x, shift, scale, diff_w_in.astype(BF16), _rope_tables(T),
                         blocks["proj_tm"], 1024)
    return diff_attention(qkv, diff_lambda, diff_subln_g, lambda_init, blocks["attn_tq"])


def _blocks(T):
    return {
        "proj_tm": min(1024, T),
        "pre_tm": min(256, T),
        "post_tm": min(1024, T),
        "rwkv_tl": min(512, T),
        "scan_tl": min(512, T),
        "attn_tq": min(256, T),
        "ln_tm": min(512, T),
        "ffn_tm": min(512, T),
    }


def kernel(x, c, ada_w, ada_b, ln_g, ln_b, ffn_w_in, ffn_w_out, ab_w_in, ab_w_out, rwkv_mu, rwkv_w0, rwkv_w_up, rwkv_a0, rwkv_a_up, rwkv_g_up, rwkv_k_k, rwkv_k_a, rwkv_r_k, rwkv_gn_g, rwkv_gn_b, gla_gate_up, gla_gate_b, gla_norm_g, diff_w_in, diff_w_out, diff_lambda, diff_subln_g):
    blocks = _blocks(x.shape[1])
    shift, scale, gate = ada_modulation(c, ada_w, ada_b)
    ffn_w_in_b = ffn_w_in.astype(BF16).reshape(DEPTH, D_MODEL, 2 * D_FF // FFN_TF, FFN_TF)
    ffn_w_in_b = ffn_w_in_b.transpose(0, 2, 1, 3)
    ffn_w_out_b = ffn_w_out.astype(BF16)
    for i in range(DEPTH):
        j = i // 2
        m = 2 * i
        if i % 2 == 0:
            y_parts = _mixer_ab(
                x, shift[m], scale[m], ab_w_in[j], rwkv_mu[j], rwkv_w0[j], rwkv_w_up[j],
                rwkv_a0[j], rwkv_a_up[j], rwkv_g_up[j], rwkv_k_k[j], rwkv_k_a[j], rwkv_r_k[j],
                rwkv_gn_g[j], rwkv_gn_b[j], gla_gate_up[j], gla_gate_b[j], gla_norm_g[j], blocks)
            w_out = ab_w_out[j]
        else:
            y_parts = (_mixer_c(x, shift[m], scale[m], diff_w_in[j], diff_lambda[j],
                                diff_subln_g[j], _lambda_init(i), blocks),)
            w_out = diff_w_out[j]
        x = projection_layernorm(y_parts, w_out.astype(BF16), x, gate[m], ln_g[i, 0], ln_b[i, 0],
                                 blocks["ln_tm"])
        x = ffn_sublayer(x, shift[m + 1], scale[m + 1], gate[m + 1], ffn_w_in_b, ffn_w_out_b, i,
                         ln_g[i, 1], ln_b[i, 1], blocks["ffn_tm"])
    return x
```

```python
import functools
import math

import jax
import jax.numpy as jnp
from jax import lax
from jax.experimental import pallas as pl
from jax.experimental.pallas import tpu as pltpu

F32 = jnp.float32
BF16 = jnp.bfloat16
HI = lax.Precision.HIGHEST

D_MODEL = 2048
DEPTH = 2
ALPHA = (2.0 * DEPTH) ** 0.25
LN_EPS = 1e-5

RWKV_HEAD_DIM = 64
RWKV_DIM = D_MODEL // 2
RWKV_HEADS = RWKV_DIM // RWKV_HEAD_DIM
DECAY_LORA = 96
ICL_LORA = 96
GATE_LORA = 256
W_DECAY_SCALE = 0.606531
RWKV_GN_EPS = 64e-5

GLA_HEADS = 4
GLA_VDIM = D_MODEL // 2
GLA_KDIM = GLA_VDIM // 2
GLA_DK = GLA_KDIM // GLA_HEADS
GLA_DV = GLA_VDIM // GLA_HEADS
GLA_GATE_LORA = 16
GLA_GATE_NORMALIZER = 16.0

DIFF_HEAD_DIM = 64
DIFF_V_DIM = 2 * DIFF_HEAD_DIM
DIFF_HEADS = D_MODEL // DIFF_V_DIM
ROPE_THETA = 10000.0

D_FF = -(-(8 * D_MODEL) // (3 * 256)) * 256

LANES = 128
VMEM_LIMIT = 56 * 1024 * 1024

CHUNK = 64
ROPE_SUB_ROWS = 256
FFN_SUB_ROWS = 256
LN_SUB_ROWS = 128
GLA_SUB_ROWS = 256
LORA_PAD = 128
HALO_ROWS = 16
Z_SEGS = {
    "rkv": (0, 3 * RWKV_DIM, 0),
    "wd": (3 * RWKV_DIM, DECAY_LORA, 3072),
    "ad": (3 * RWKV_DIM + DECAY_LORA, ICL_LORA, 3200),
    "gd": (3 * RWKV_DIM + DECAY_LORA + ICL_LORA, GATE_LORA, 3328),
    "q": (3520, GLA_KDIM, 3584),
    "k": (3520 + GLA_KDIM, GLA_KDIM, 4096),
    "v": (3520 + 2 * GLA_KDIM, GLA_VDIM, 4608),
    "gg": (3520 + 2 * GLA_KDIM + GLA_VDIM, GLA_GATE_LORA, 6656),
    "og": (3520 + 2 * GLA_KDIM + GLA_VDIM + GLA_GATE_LORA, GLA_VDIM, 5632),
}
Z_A_COLS = 3584
Z_COLS = 7168


def _cparams(sem):
    return pltpu.CompilerParams(dimension_semantics=sem, vmem_limit_bytes=VMEM_LIMIT)


def _dot(a, b, prec=None):
    return jnp.dot(a, b, preferred_element_type=F32, precision=prec)


def _dot_nt(a, b, prec=None):
    return lax.dot_general(a, b, (((1,), (1,)), ((), ())), preferred_element_type=F32,
                           precision=prec)


def _dot_tn(a, b, prec=None):
    return lax.dot_general(a, b, (((0,), (0,)), ((), ())), preferred_element_type=F32,
                           precision=prec)


def _sigmoid(x):
    return 1.0 / (1.0 + jnp.exp(-x))


def _layer_norm_rows(u, g, b, eps):
    mu = jnp.mean(u, axis=-1, keepdims=True)
    d = u - mu
    var = jnp.mean(d * d, axis=-1, keepdims=True)
    return d * lax.rsqrt(var + eps) * g + b


def _ada_body(c_ref, w_ref, b_ref, o_ref):
    c = c_ref[...]
    sc = (c * _sigmoid(c)).astype(BF16)
    o_ref[0] = _dot(sc, w_ref[0].astype(BF16)) + b_ref[0]


def ada_modulation(c, ada_w, ada_b):
    B, D = c.shape
    n = ada_w.shape[0] * ada_w.shape[1]
    w = ada_w.reshape(n, D, 3 * D)
    b = ada_b.reshape(n, 1, 3 * D)
    rows = 8
    c_pad = jnp.pad(c, ((0, rows - B), (0, 0)))
    tn = 1024
    out = pl.pallas_call(
        _ada_body,
        grid=(n, 3 * D // tn),
        in_specs=[
            pl.BlockSpec((rows, D), lambda i, j: (0, 0)),
            pl.BlockSpec((1, D, tn), lambda i, j: (i, 0, j)),
            pl.BlockSpec((1, 1, tn), lambda i, j: (i, 0, j)),
        ],
        out_specs=pl.BlockSpec((1, rows, tn), lambda i, j: (i, 0, j)),
        out_shape=jax.ShapeDtypeStruct((n, rows, 3 * D), F32),
        compiler_params=_cparams(("parallel", "parallel")),
        name="ada_modulation",
    )(c_pad, w, b)
    mods = out[:, :B, :]
    shift, scale, gate = mods[..., :D], mods[..., D:2 * D], mods[..., 2 * D:]
    r3 = lambda t: t.reshape(n, B, 1, D)
    return r3(shift), r3(scale), r3(gate)


def _inproj_body(x_ref, sh_ref, sc_ref, w_ref, o_ref, h_ref):
    @pl.when(pl.program_id(2) == 0)
    def _():
        h_ref[...] = (x_ref[0] * (1.0 + sc_ref[0]) + sh_ref[0]).astype(BF16)

    o_ref[0] = _dot(h_ref[...], w_ref[...]).astype(o_ref.dtype)


def modulated_projection(x, shift, scale, w, out_dtype, tm, tn):
    B, T, D = x.shape
    N = w.shape[1]
    return pl.pallas_call(
        _inproj_body,
        grid=(B, T // tm, N // tn),
        in_specs=[
            pl.BlockSpec((1, tm, D), lambda b, i, j: (b, i, 0)),
            pl.BlockSpec((1, 1, D), lambda b, i, j: (b, 0, 0)),
            pl.BlockSpec((1, 1, D), lambda b, i, j: (b, 0, 0)),
            pl.BlockSpec((D, tn), lambda b, i, j: (0, j)),
        ],
        out_specs=pl.BlockSpec((1, tm, tn), lambda b, i, j: (b, i, j)),
        out_shape=jax.ShapeDtypeStruct((B, T, N), out_dtype),
        scratch_shapes=[pltpu.VMEM((tm, D), BF16)],
        compiler_params=_cparams(("parallel", "parallel", "arbitrary")),
        name="modulated_projection",
    )(x, shift, scale, w)


def _qkv_body(x_ref, sh_ref, sc_ref, w_ref, cos_ref, sup_ref, sdn_ref, o_ref, h_ref, *, n_q, n_qk,
              q_scale):
    j = pl.program_id(2)

    @pl.when(j == 0)
    def _():
        h_ref[...] = (x_ref[0] * (1.0 + sc_ref[0]) + sh_ref[0]).astype(BF16)

    is_qk = j < n_qk
    mult = jnp.where(j < n_q, q_scale, 1.0).astype(F32)
    cos = jnp.where(is_qk, cos_ref[...] * mult, 1.0)
    sup = jnp.where(is_qk, sup_ref[...] * mult, 0.0)
    sdn = jnp.where(is_qk, sdn_ref[...] * mult, 0.0)
    half = DIFF_HEAD_DIM // 2
    tm, tn = o_ref.shape[1], o_ref.shape[2]
    sub = min(ROPE_SUB_ROWS, tm)
    blocks = [slice(s * sub, (s + 1) * sub) for s in range(tm // sub)]
    accs = [_dot(h_ref[rows, :], w_ref[...]) for rows in blocks]
    for rows, acc in zip(blocks, accs):
        for s in range(tn // LANES):
            xs = acc[:, s * LANES:(s + 1) * LANES]
            up = pltpu.roll(xs, LANES - half, axis=1)
            dn = pltpu.roll(xs, half, axis=1)
            o_ref[0, rows, s * LANES:(s + 1) * LANES] = (
                xs * cos[rows] + up * sup[rows] + dn * sdn[rows]).astype(o_ref.dtype)


def qkv_projection(x, shift, scale, w, rope_tables, tm, tn):
    B, T, D = x.shape
    N = w.shape[1]
    n_q = D_MODEL // tn
    body = functools.partial(_qkv_body, n_q=n_q, n_qk=2 * n_q,
                             q_scale=DIFF_HEAD_DIM ** -0.5 * math.log2(math.e))
    return pl.pallas_call(
        body,
        grid=(B, T // tm, N // tn),
        in_specs=[
            pl.BlockSpec((1, tm, D), lambda b, i, j: (b, i, 0)),
            pl.BlockSpec((1, 1, D), lambda b, i, j: (b, 0, 0)),
            pl.BlockSpec((1, 1, D), lambda b, i, j: (b, 0, 0)),
            pl.BlockSpec((D, tn), lambda b, i, j: (0, j)),
            pl.BlockSpec((tm, LANES), lambda b, i, j: (i, 0)),
            pl.BlockSpec((tm, LANES), lambda b, i, j: (i, 0)),
            pl.BlockSpec((tm, LANES), lambda b, i, j: (i, 0)),
        ],
        out_specs=pl.BlockSpec((1, tm, tn), lambda b, i, j: (b, i, j)),
        out_shape=jax.ShapeDtypeStruct((B, T, N), BF16),
        scratch_shapes=[pltpu.VMEM((tm, D), BF16)],
        compiler_params=_cparams(("parallel", "parallel", "arbitrary")),
        name="qkv_projection",
    )(x, shift, scale, w, *rope_tables)


def _proj_ln_body(*refs, n_parts):
    y_refs = refs[:n_parts]
    w_ref, x_ref, gate_ref, g_ref, b_ref, o_ref = refs[n_parts:]
    tm = x_ref.shape[1]
    sub = min(LN_SUB_ROWS, tm)
    accs = []
    for s in range(tm // sub):
        rows = slice(s * sub, (s + 1) * sub)
        acc = None
        off = 0
        for yr in y_refs:
            k = yr.shape[-1]
            part = _dot(yr[0, rows, :], w_ref[off:off + k, :])
            acc = part if acc is None else acc + part
            off += k
        accs.append(acc)
    for s, acc in enumerate(accs):
        rows = slice(s * sub, (s + 1) * sub)
        u = ALPHA * x_ref[0, rows, :] + (1.0 + gate_ref[0]) * acc
        o_ref[0, rows, :] = _layer_norm_rows(u, g_ref[...], b_ref[...], LN_EPS)


def projection_layernorm(y_parts, w, x, gate, ln_g, ln_b, tm):
    B, T, D = x.shape
    K = w.shape[0]
    in_specs = [pl.BlockSpec((1, tm, yp.shape[-1]), lambda b, i: (b, i, 0)) for yp in y_parts]
    in_specs += [
        pl.BlockSpec((K, D), lambda b, i: (0, 0)),
        pl.BlockSpec((1, tm, D), lambda b, i: (b, i, 0)),
        pl.BlockSpec((1, 1, D), lambda b, i: (b, 0, 0)),
        pl.BlockSpec((1, D), lambda b, i: (0, 0)),
        pl.BlockSpec((1, D), lambda b, i: (0, 0)),
    ]
    return pl.pallas_call(
        functools.partial(_proj_ln_body, n_parts=len(y_parts)),
        grid=(B, T // tm),
        in_specs=in_specs,
        out_specs=pl.BlockSpec((1, tm, D), lambda b, i: (b, i, 0)),
        out_shape=jax.ShapeDtypeStruct((B, T, D), F32),
        compiler_params=_cparams(("parallel", "parallel")),
        name="projection_layernorm",
    )(*y_parts, w, x, gate, ln_g.reshape(1, D), ln_b.reshape(1, D))


def _ffn_body(x_ref, sh_ref, sc_ref, gate_ref, wg_ref, wu_ref, wo_ref, g_ref, b_ref, o_ref,
              h_ref, acc_ref):
    j = pl.program_id(2)
    last = pl.num_programs(2) - 1
    tm = x_ref.shape[1]
    sub = min(FFN_SUB_ROWS, tm)
    blocks = [slice(s * sub, (s + 1) * sub) for s in range(tm // sub)]

    def contribution(h):
        gt = _dot(h, wg_ref[0])
        up = _dot(h, wu_ref[0])
        act = (gt * _sigmoid(gt) * up).astype(BF16)
        return _dot(act, wo_ref[0])

    @pl.when(j == 0)
    def _():
        for rows in blocks:
            h = (x_ref[0, rows, :] * (1.0 + sc_ref[0]) + sh_ref[0]).astype(BF16)
            h_ref[rows, :] = h
            acc_ref[rows, :] = contribution(h)

    @pl.when((j > 0) & (j < last))
    def _():
        acc_ref[...] += contribution(h_ref[...])

    @pl.when(j == last)
    def _():
        for rows in blocks:
            acc = acc_ref[rows, :] + contribution(h_ref[rows, :])
            u = ALPHA * x_ref[0, rows, :] + (1.0 + gate_ref[0]) * acc
            o_ref[0, rows, :] = _layer_norm_rows(u, g_ref[...], b_ref[...], LN_EPS)


def ffn_sublayer(x, shift, scale, gate, w_in, w_out, layer, ln_g, ln_b, tm, tf):
    B, T, D = x.shape
    F = w_out.shape[1]
    nf = F // tf
    assert nf >= 2, "the first and last F-steps must be distinct grid steps"
    return pl.pallas_call(
        _ffn_body,
        grid=(B, T // tm, nf),
        in_specs=[
            pl.BlockSpec((1, tm, D), lambda b, i, j: (b, i, 0)),
            pl.BlockSpec((1, 1, D), lambda b, i, j: (b, 0, 0)),
            pl.BlockSpec((1, 1, D), lambda b, i, j: (b, 0, 0)),
            pl.BlockSpec((1, 1, D), lambda b, i, j: (b, 0, 0)),
            pl.BlockSpec((1, D, tf), lambda b, i, j: (layer, 0, j)),
            pl.BlockSpec((1, D, tf), lambda b, i, j: (layer, 0, j + nf)),
            pl.BlockSpec((1, tf, D), lambda b, i, j: (layer, j, 0)),
            pl.BlockSpec((1, D), lambda b, i, j: (0, 0)),
            pl.BlockSpec((1, D), lambda b, i, j: (0, 0)),
        ],
        out_specs=pl.BlockSpec((1, tm, D), lambda b, i, j: (b, i, 0)),
        out_shape=jax.ShapeDtypeStruct((B, T, D), F32),
        scratch_shapes=[pltpu.VMEM((tm, D), BF16), pltpu.VMEM((tm, D), F32)],
        compiler_params=_cparams(("parallel", "parallel", "arbitrary")),
        name="ffn_sublayer",
    )(x, shift, scale, gate, w_in, w_in, w_out, ln_g.reshape(1, D), ln_b.reshape(1, D))


ONES_ROWS = 16


KEY_CHUNK = 512


def _attn_body(q0_ref, qn_ref, kn_ref, v_ref, lam_ref, g_ref, o_ref, vt_ref, s_ref, m_ref,
               *, lambda_init):
    dv = DIFF_V_DIM
    T = kn_ref.shape[1]
    first_tile = pl.program_id(2) == 0
    first_step = first_tile & (pl.program_id(0) == 0) & (pl.program_id(1) == 0)
    lane = lax.broadcasted_iota(jnp.int32, (1, LANES), 1)
    comp0 = lane < DIFF_HEAD_DIM

    def comps(q):
        zero = jnp.zeros_like(q)
        return (jnp.where(comp0, q, zero), jnp.where(comp0, zero, q))

    def chunk_scores(qcs, rows):
        kc = kn_ref[0, rows, :]
        return [_dot_nt(kc, qc) for qc in qcs]

    def col_max(s):
        return jnp.max(s, axis=0, keepdims=True)

    chunks = [slice(c * KEY_CHUNK, (c + 1) * KEY_CHUNK) for c in range(T // KEY_CHUNK)]

    @pl.when(first_tile)
    def _():
        vt_ref[0:dv, :] = v_ref[0].astype(F32).T.astype(BF16)
        vt_ref[dv:, :] = jnp.ones((ONES_ROWS, T), BF16)

    @pl.when(first_step)
    def _():
        qcs = comps(q0_ref[0])
        m = [None, None]
        for rows in chunks:
            s = chunk_scores(qcs, rows)
            for j in range(2):
                s_ref[j, rows, :] = s[j]
                m[j] = col_max(s[j]) if m[j] is None else jnp.maximum(m[j], col_max(s[j]))
        for j in range(2):
            m_ref[j] = m[j]

    qcs = comps(qn_ref[0])
    m_cur = [m_ref[0], m_ref[1]]
    m_new = [None, None]
    acc = [None, None]
    for rows in chunks:
        es = [jnp.exp2(s_ref[j, rows, :] - m_cur[j]).astype(BF16) for j in range(2)]
        for j in range(2):
            part = _dot(vt_ref[:, rows], es[j])
            acc[j] = part if acc[j] is None else acc[j] + part
        s_new = chunk_scores(qcs, rows)
        for j in range(2):
            s_ref[j, rows, :] = s_new[j]
            m_new[j] = col_max(s_new[j]) if m_new[j] is None else jnp.maximum(m_new[j], col_max(s_new[j]))
    for j in range(2):
        m_ref[j] = m_new[j]

    lp = lam_ref[...]
    lam = (jnp.exp(jnp.sum(lp[0:1] * lp[1:2], axis=-1, keepdims=True))
           - jnp.exp(jnp.sum(lp[2:3] * lp[3:4], axis=-1, keepdims=True)) + lambda_init)
    outs = [a[0:dv] / a[dv:dv + 1] for a in acc]
    o = (outs[0] - lam * outs[1]).T
    ms = jnp.mean(o * o, axis=-1, keepdims=True)
    o = o * lax.rsqrt(ms + LN_EPS) * g_ref[...] * (1.0 - lambda_init)
    o_ref[0] = o.astype(o_ref.dtype)


def diff_attention(qkv, lam_params, subln_g, lambda_init, tq):
    B, T, _ = qkv.shape
    H = DIFF_HEADS
    nq = T // tq

    def nxt(b, h, i):
        lin = jnp.minimum((b * H + h) * nq + i + 1, B * H * nq - 1)
        return lin // (H * nq), (lin // nq) % H, lin % nq

    def qn_map(b, h, i):
        nb, nh, nt = nxt(b, h, i)
        return nb, nt, nh

    def kn_map(b, h, i):
        nb, nh, _ = nxt(b, h, i)
        return nb, 0, H + nh

    return pl.pallas_call(
        functools.partial(_attn_body, lambda_init=lambda_init),
        grid=(B, H, nq),
        in_specs=[
            pl.BlockSpec((1, tq, LANES), lambda b, h, i: (0, 0, 0)),
            pl.BlockSpec((1, tq, LANES), qn_map),
            pl.BlockSpec((1, T, LANES), kn_map),
            pl.BlockSpec((1, T, LANES), lambda b, h, i: (b, 0, 2 * H + h)),
            pl.BlockSpec((4, DIFF_HEAD_DIM), lambda b, h, i: (0, 0)),
            pl.BlockSpec((1, LANES), lambda b, h, i: (0, 0)),
        ],
        out_specs=pl.BlockSpec((1, tq, LANES), lambda b, h, i: (b, i, h)),
        out_shape=jax.ShapeDtypeStruct((B, T, H * DIFF_V_DIM), BF16),
        scratch_shapes=[pltpu.VMEM((DIFF_V_DIM + ONES_ROWS, T), BF16),
                        pltpu.VMEM((2, T, tq), F32),
                        pltpu.VMEM((2, 1, tq), F32)],
        compiler_params=_cparams(("arbitrary", "arbitrary", "arbitrary")),
        name="diff_attention",
    )(qkv, qkv, qkv, qkv, lam_params, subln_g.reshape(1, LANES))


def _per_head_sum(x, head_dim):
    lane = lax.broadcasted_iota(jnp.int32, (1, LANES), 1)
    lo = lane < head_dim
    s_lo = jnp.sum(jnp.where(lo, x, 0.0), axis=-1, keepdims=True)
    s_hi = jnp.sum(jnp.where(lo, 0.0, x), axis=-1, keepdims=True)
    return jnp.where(lo, s_lo, s_hi)


def _rwkv_pre_body(z_ref, zp_ref, zn_ref, mu_ref, w0_ref, wup_ref, a0_ref, aup_ref, gup_ref,
                   kk_ref, ka_ref, r_o, k_o, v_o, kk_o, ba_o, lw_o, g_o, *, tm):
    i = pl.program_id(1)
    last = pl.num_programs(1) - 1
    row8 = lax.broadcasted_iota(jnp.int32, (8, 1), 0)
    C = RWKV_DIM

    def shifted(lo, hi):
        z = z_ref[0, :, lo:hi].astype(F32)
        prev = jnp.where(i == 0, 0.0, zp_ref[0, HALO_ROWS - 1:HALO_ROWS, lo:hi].astype(F32))
        nxt = jnp.where(i == last, 0.0, zn_ref[0, 0:1, lo:hi].astype(F32))
        nb = pltpu.roll(z, 1, axis=0) + pltpu.roll(z, tm - 1, axis=0)
        top = nb[0:8] + jnp.where(row8 == 0, prev - z[tm - 1:tm], 0.0)
        bot = nb[tm - 8:tm] + jnp.where(row8 == 7, nxt - z[0:1], 0.0)
        nb = jnp.concatenate([top, nb[8:tm - 8], bot], axis=0)
        mu = mu_ref[:, lo:hi]
        return z * (1.0 - mu) + nb * (0.5 * mu)

    r = shifted(0, C)
    k = shifted(C, 2 * C)
    v = shifted(2 * C, 3 * C)
    wd = shifted(3 * C, 3 * C + LORA_PAD)
    ad = shifted(3 * C + LORA_PAD, 3 * C + 2 * LORA_PAD)
    gd = shifted(3 * C + 2 * LORA_PAD, 3 * C + 2 * LORA_PAD + GATE_LORA)

    r_o[0] = r.astype(r_o.dtype)
    v_o[0] = v.astype(v_o.dtype)
    twd = jnp.tanh(wd).astype(BF16)
    for d in range(2):
        lw_o[d, 0] = -W_DECAY_SCALE * _sigmoid(w0_ref[d] + _dot(twd, wup_ref[d].astype(BF16)))
    a = _sigmoid(a0_ref[...] + _dot(ad.astype(BF16), aup_ref[...].astype(BF16)))
    g_o[0] = _dot(_sigmoid(gd).astype(BF16), gup_ref[...].astype(BF16)).astype(g_o.dtype)
    k_o[0] = (k * (1.0 + (a - 1.0) * ka_ref[...])).astype(k_o.dtype)
    kk0 = k * kk_ref[...]
    for s in range(C // LANES):
        sl = slice(s * LANES, (s + 1) * LANES)
        x = kk0[:, sl]
        nrm = jnp.maximum(jnp.sqrt(_per_head_sum(x * x, RWKV_HEAD_DIM)), 1e-12)
        kk = x / nrm
        kk_o[0, :, sl] = kk.astype(kk_o.dtype)
        ba_o[0, :, sl] = (kk * a[:, sl]).astype(ba_o.dtype)


def rwkv_pre(z, mu_p, w0, wup_p, a0, aup_p, gup, k_k, k_a, tm):
    B, T, _ = z.shape
    C = RWKV_DIM
    nbh = tm // HALO_ROWS
    nh = T // HALO_ROWS
    bt = jax.ShapeDtypeStruct((B, T, C), BF16)
    vec = lambda t: t.reshape(1, C)
    full = lambda shp: pl.BlockSpec(shp, lambda b, i: (0,) * len(shp))
    body = functools.partial(_rwkv_pre_body, tm=tm)
    return pl.pallas_call(
        body,
        grid=(B, T // tm),
        in_specs=[
            pl.BlockSpec((1, tm, Z_A_COLS), lambda b, i: (b, i, 0)),
            pl.BlockSpec((1, HALO_ROWS, Z_A_COLS), lambda b, i: (b, jnp.maximum(i * nbh - 1, 0), 0)),
            pl.BlockSpec((1, HALO_ROWS, Z_A_COLS),
                         lambda b, i: (b, jnp.minimum((i + 1) * nbh, nh - 1), 0)),
            full((1, Z_A_COLS)),
            full((2, 1, C)),
            full((2, LORA_PAD, C)),
            full((1, C)),
            full((LORA_PAD, C)),
            full((GATE_LORA, C)),
            full((1, C)),
            full((1, C)),
        ],
        out_specs=[pl.BlockSpec((1, tm, C), lambda b, i: (b, i, 0))] * 5
        + [pl.BlockSpec((2, 1, tm, C), lambda b, i: (0, b, i, 0)),
           pl.BlockSpec((1, tm, C), lambda b, i: (b, i, 0))],
        out_shape=[bt] * 5 + [jax.ShapeDtypeStruct((2, B, T, C), F32), bt],
        compiler_params=_cparams(("parallel", "parallel")),
        name="rwkv_pre",
    )(z, z, z, mu_p, w0.reshape(2, 1, C), wup_p, vec(a0), aup_p, gup, vec(k_k), vec(k_a))


def _bf(x):
    return x.astype(BF16)


def _segmented_cumsum(x, seg):
    pos = lax.broadcasted_iota(jnp.int32, (x.shape[0], 1), 0) % seg
    s = 1
    while s < seg:
        x = x + jnp.where(pos >= s, pltpu.roll(x, s, axis=0), 0.0)
        s *= 2
    return x


def _segment_totals(x, seg):
    parts = []
    for c in range(x.shape[0] // seg):
        t = jnp.sum(x[c * seg:(c + 1) * seg], axis=0, keepdims=True)
        parts.append(jnp.broadcast_to(t, (seg, x.shape[1])))
    return jnp.concatenate(parts, axis=0)


def _decay_factors(lw, reverse):
    tot = _segment_totals(lw, CHUNK)
    cum = _segmented_cumsum(lw, CHUNK)
    if reverse:
        cum = tot - cum + lw
    return cum, tot


def _rwkv_chunks_local(chunks):
    L = CHUNK
    lane = lax.broadcasted_iota(jnp.int32, (1, LANES), 1)
    h0 = lane < RWKV_HEAD_DIM

    def stack(x):
        return jnp.concatenate([jnp.where(h0, x, 0.0), jnp.where(h0, 0.0, x)], axis=0)

    pre = []
    for r, k, v, kk, ba, lw, cum, tot, strict, incl in chunks:
        g_ex = jnp.exp(cum - lw)
        g_inv = jnp.exp(-cum)
        g_rem = jnp.exp(tot - cum)
        a_st = stack(-kk * g_ex)
        r_st = stack(r * jnp.exp(cum))
        lhs = _bf(jnp.concatenate([a_st, r_st], axis=0))
        rhs = _bf(jnp.concatenate([stack(ba * g_inv), stack(k * g_inv)], axis=0))
        t_b = _bf(jnp.concatenate([stack(ba * g_rem), stack(k * g_rem)], axis=0))
        pre.append((a_st, r_st, lhs, rhs, t_b, _bf(stack(v))))

    ps = [_dot_nt(lhs, rhs) for _, _, lhs, rhs, _, _ in pre]
    m_ab, m_ak, m_r = [], [], []
    for p, ch in zip(ps, chunks):
        strict, incl = ch[8], ch[9]
        m_ab.append(jnp.where(strict, p[:2 * L, :2 * L], 0.0))
        m_ak.append(_bf(jnp.where(strict, p[:2 * L, 2 * L:], 0.0)))
        m_r.append(_bf(jnp.concatenate([jnp.where(incl, p[2 * L:, :2 * L], 0.0),
                                        jnp.where(incl, p[2 * L:, 2 * L:], 0.0)], axis=1)))

    akv = [_dot(m, pr[5]) for m, pr in zip(m_ak, pre)]
    xs = [jnp.concatenate([pr[0], t], axis=1) for pr, t in zip(pre, akv)]
    ns = m_ab
    steps = int(math.log2(L))
    for it in range(steps):
        n_bs = [_bf(n) for n in ns]
        xs = [x + _dot(n_b, _bf(x)) for x, n_b in zip(xs, n_bs)]
        if it + 1 < steps:
            ns = [_dot(n_b, n_b) for n_b in n_bs]

    out = []
    wu = [(_bf(x[:, :LANES]), _bf(x[:, LANES:])) for x in xs]
    qys = [_dot(m, jnp.concatenate([jnp.concatenate([w_b, ul_b], axis=1),
                                    jnp.concatenate([jnp.zeros_like(pr[5]), pr[5]], axis=1)], axis=0))
           for m, (w_b, ul_b), pr in zip(m_r, wu, pre)]
    gps = [_dot_tn(w_b, pr[4][:2 * L]) for (w_b, _), pr in zip(wu, pre)]
    hs = [_dot_tn(jnp.concatenate([ul_b, pr[5]], axis=0), pr[4])
          for (_, ul_b), pr in zip(wu, pre)]
    for qy, gp, h, pr in zip(qys, gps, hs, pre):
        out.append((_bf(pr[1] + qy[:, :LANES]), qy[:, LANES:], _bf(gp), h))
    return out


def _rwkv_scan_body(rf, kf, vf, kkf, baf, lwf, rb, kb, vb, kkb, bab, lwb, yf_ref, yb_ref, s_ref,
                    *, n_chunks):
    L = CHUNK

    @pl.when(pl.program_id(2) == 0)
    def _():
        s_ref[...] = jnp.zeros_like(s_ref)

    r2 = lax.broadcasted_iota(jnp.int32, (2 * L, 2 * L), 0)
    c2 = lax.broadcasted_iota(jnp.int32, (2 * L, 2 * L), 1)
    same_head = (r2 // L) == (c2 // L)
    dt = r2 % L - c2 % L

    chunks, g_tots = [], []
    for d, refs in ((0, (rf, kf, vf, kkf, baf, lwf)), (1, (rb, kb, vb, kkb, bab, lwb))):
        r_ref, k_ref, v_ref, kk_ref, ba_ref, lw_ref = refs
        strict = jnp.where(same_head, dt if d == 0 else -dt, -1) > 0
        incl = strict | (r2 == c2)
        lw_all = lw_ref[0, 0]
        cum_all, tot_all = _decay_factors(lw_all, reverse=(d == 1))
        g_tot_all = jnp.exp(tot_all)
        for c in range(n_chunks):
            rows = slice(c * L, (c + 1) * L)
            chunks.append((r_ref[0, rows, :], k_ref[0, rows, :], v_ref[0, rows, :],
                           kk_ref[0, rows, :], ba_ref[0, rows, :], lw_all[rows], cum_all[rows],
                           tot_all[rows], strict, incl))
            g_tots.append(g_tot_all[c * L:c * L + 1])
    local = _rwkv_chunks_local(chunks)

    s2 = [s_ref[0], s_ref[1]]
    for j in range(n_chunks):
        for d, y_ref in ((0, yf_ref), (1, yb_ref)):
            c = j if d == 0 else n_chunks - 1 - j
            q_b, yl_st, gp_b, h = local[d * n_chunks + c]
            s_b = _bf(s2[d])
            y_st = _dot_nt(q_b, s_b) + yl_st
            y_ref[0, c * L:(c + 1) * L, :] = y_st[:L] + y_st[L:]
            s2[d] = s2[d] * g_tots[d * n_chunks + c] + _dot(s_b, gp_b) + h
    s_ref[0] = s2[0]
    s_ref[1] = s2[1]


def rwkv_scan(r, k, v, kk, ba, lw, tl):
    B, T, C = r.shape
    n_pairs = C // LANES
    nblk = T // tl
    fwd = pl.BlockSpec((1, tl, LANES), lambda b, p, i: (b, i, p))
    bwd = pl.BlockSpec((1, tl, LANES), lambda b, p, i: (b, nblk - 1 - i, p))
    lw_f = pl.BlockSpec((1, 1, tl, LANES), lambda b, p, i: (0, b, i, p))
    lw_b = pl.BlockSpec((1, 1, tl, LANES), lambda b, p, i: (1, b, nblk - 1 - i, p))
    y_sds = jax.ShapeDtypeStruct((B, T, C), F32)
    return pl.pallas_call(
        functools.partial(_rwkv_scan_body, n_chunks=tl // CHUNK),
        grid=(B, n_pairs, nblk),
        in_specs=[fwd] * 5 + [lw_f] + [bwd] * 5 + [lw_b],
        out_specs=[fwd, bwd],
        out_shape=[y_sds, y_sds],
        scratch_shapes=[pltpu.VMEM((2, LANES, LANES), F32)],
        compiler_params=_cparams(("parallel", "parallel", "arbitrary")),
        name="rwkv_scan",
    )(r, k, v, kk, ba, lw, r, k, v, kk, ba, lw)


def _rwkv_post_body(yf_ref, yb_ref, r_ref, k_ref, v_ref, g_ref, rk_ref, gg_ref, gb_ref, o_ref):
    C = RWKV_DIM
    for s in range(C // LANES):
        sl = slice(s * LANES, (s + 1) * LANES)
        y = yf_ref[0, :, sl] + yb_ref[0, :, sl]
        inv_n = 1.0 / RWKV_HEAD_DIM
        mu = _per_head_sum(y, RWKV_HEAD_DIM) * inv_n
        dlt = y - mu
        var = _per_head_sum(dlt * dlt, RWKV_HEAD_DIM) * inv_n
        yn = dlt * lax.rsqrt(var + RWKV_GN_EPS) * gg_ref[:, sl] + gb_ref[:, sl]
        rk = r_ref[0, :, sl].astype(F32) * k_ref[0, :, sl].astype(F32)
        bonus = _per_head_sum(rk * rk_ref[:, sl], RWKV_HEAD_DIM)
        out = (yn + bonus * v_ref[0, :, sl].astype(F32)) * g_ref[0, :, sl].astype(F32)
        o_ref[0, :, sl] = out.astype(o_ref.dtype)


def rwkv_post(yf, yb, r, k, v, g, r_k, gn_g, gn_b, tm):
    B, T, C = r.shape
    spec = pl.BlockSpec((1, tm, C), lambda b, i: (b, i, 0))
    vspec = pl.BlockSpec((1, C), lambda b, i: (0, 0))
    return pl.pallas_call(
        _rwkv_post_body,
        grid=(B, T // tm),
        in_specs=[spec] * 6 + [vspec] * 3,
        out_specs=spec,
        out_shape=jax.ShapeDtypeStruct((B, T, C), BF16),
        compiler_params=_cparams(("parallel", "parallel")),
        name="rwkv_post",
    )(yf, yb, r, k, v, g, r_k.reshape(1, C), gn_g.reshape(1, C), gn_b.reshape(1, C))


def _log_sigmoid(x):
    return jnp.minimum(x, 0.0) - jnp.log(1.0 + jnp.exp(-jnp.abs(x)))


def _gla_body(qf, kf, vf, ggf, qb_ref, kb_ref, vb_ref, ggb, gup_ref, gb_ref, of_ref, ob_ref, st_ref,
              *, n_chunks):
    L = CHUNK
    tl = n_chunks * L

    @pl.when(pl.program_id(2) == 0)
    def _():
        st_ref[...] = jnp.zeros_like(st_ref)

    sub = min(GLA_SUB_ROWS, tl)
    ri = lax.broadcasted_iota(jnp.int32, (sub, sub), 0)
    ci = lax.broadcasted_iota(jnp.int32, (sub, sub), 1)
    same_chunk = (ri // L) == (ci // L)
    chunk_rows = [slice(c * L, (c + 1) * L) for c in range(n_chunks)]

    loc = []
    for d, (q_ref, k_ref, v_ref, gg_ref) in enumerate(((qf, kf, vf, ggf), (qb_ref, kb_ref, vb_ref, ggb))):
        lg = _log_sigmoid(_dot(_bf(gg_ref[0]), _bf(gup_ref[d])) + gb_ref[d]) * (1.0 / GLA_GATE_NORMALIZER)
        cum, tot = _decay_factors(lg, reverse=(d == 1))
        k = k_ref[0].astype(F32)
        loc.append(dict(
            v=_bf(v_ref[0]),
            qb=_bf(q_ref[0].astype(F32) * (GLA_DK ** -0.5) * jnp.exp(cum)),
            kb=_bf(k * jnp.exp(-cum)),
            kt=_bf(k * jnp.exp(tot - cum)),
            g_tot=jnp.exp(tot),
            incl=jnp.where(same_chunk, (ri - ci) if d == 0 else (ci - ri), -1) >= 0))
    blocks = [slice(s * sub, (s + 1) * sub) for s in range(tl // sub)]
    atts = [[jnp.where(lc["incl"], _dot_nt(lc["qb"][b], lc["kb"][b]), 0.0) for b in blocks] for lc in loc]
    o_intra = [jnp.concatenate([_dot(_bf(a), lc["v"][b]) for a, b in zip(att, blocks)], axis=0)
               for att, lc in zip(atts, loc)]
    kvs = [[_dot_tn(lc["v"][r], lc["kt"][r]) for r in chunk_rows] for lc in loc]

    st = [st_ref[0], st_ref[1]]
    for j in range(n_chunks):
        for d, o_ref in ((0, of_ref), (1, ob_ref)):
            c = j if d == 0 else n_chunks - 1 - j
            rows = chunk_rows[c]
            o_ref[0, rows, :] = o_intra[d][rows] + _dot_nt(loc[d]["qb"][rows], _bf(st[d]))
            st[d] = st[d] * loc[d]["g_tot"][c * L:c * L + 1] + kvs[d][c]
    st_ref[0] = st[0]
    st_ref[1] = st[1]


def gla_scan(z, gate_up_p, gate_b, tl):
    B, T, _ = z.shape
    H = GLA_HEADS
    nblk = T // tl
    qc = Z_SEGS["q"][2] // GLA_DK
    kc = Z_SEGS["k"][2] // GLA_DK
    vc = Z_SEGS["v"][2] // GLA_DV
    gc = Z_SEGS["gg"][2] // LANES
    specs = []
    for tmap in (lambda i: i, lambda i: nblk - 1 - i):
        specs += [
            pl.BlockSpec((1, tl, GLA_DK), lambda b, h, i, tmap=tmap: (b, tmap(i), qc + h)),
            pl.BlockSpec((1, tl, GLA_DK), lambda b, h, i, tmap=tmap: (b, tmap(i), kc + h)),
            pl.BlockSpec((1, tl, GLA_DV), lambda b, h, i, tmap=tmap: (b, tmap(i), vc + h)),
            pl.BlockSpec((1, tl, LANES), lambda b, h, i, tmap=tmap: (b, tmap(i), gc)),
        ]
    o_sds = jax.ShapeDtypeStruct((B, T, GLA_VDIM), F32)
    return pl.pallas_call(
        functools.partial(_gla_body, n_chunks=tl // CHUNK),
        grid=(B, H, nblk),
        in_specs=specs + [
            pl.BlockSpec((2, LANES, GLA_DK), lambda b, h, i: (0, 0, h)),
            pl.BlockSpec((2, 1, GLA_DK), lambda b, h, i: (0, 0, h)),
        ],
        out_specs=[pl.BlockSpec((1, tl, GLA_DV), lambda b, h, i: (b, i, h)),
                   pl.BlockSpec((1, tl, GLA_DV), lambda b, h, i: (b, nblk - 1 - i, h))],
        out_shape=[o_sds, o_sds],
        scratch_shapes=[pltpu.VMEM((2, GLA_DV, GLA_DK), F32)],
        compiler_params=_cparams(("parallel", "parallel", "arbitrary")),
        name="gla_scan",
    )(z, z, z, z, z, z, z, z, gate_up_p, gate_b.reshape(2, 1, GLA_KDIM))


def _gla_post_body(of_ref, ob_ref, og_ref, g_ref, y_ref):
    o = of_ref[0] + ob_ref[0]
    ms = jnp.mean(o * o, axis=-1, keepdims=True)
    on = o * lax.rsqrt(ms + LN_EPS) * g_ref[...]
    og = og_ref[0].astype(F32)
    y_ref[0] = (on * (og * _sigmoid(og))).astype(y_ref.dtype)


def gla_post(o_f, o_b, z, norm_g, tm):
    B, T, _ = o_f.shape
    H = GLA_HEADS
    oc = Z_SEGS["og"][2] // GLA_DV
    return pl.pallas_call(
        _gla_post_body,
        grid=(B, T // tm, H),
        in_specs=[
            pl.BlockSpec((1, tm, GLA_DV), lambda b, i, h: (b, i, h)),
            pl.BlockSpec((1, tm, GLA_DV), lambda b, i, h: (b, i, h)),
            pl.BlockSpec((1, tm, GLA_DV), lambda b, i, h: (b, i, oc + h)),
            pl.BlockSpec((1, GLA_DV), lambda b, i, h: (0, 0)),
        ],
        out_specs=pl.BlockSpec((1, tm, GLA_DV), lambda b, i, h: (b, i, h)),
        out_shape=jax.ShapeDtypeStruct((B, T, GLA_VDIM), BF16),
        compiler_params=_cparams(("parallel", "parallel", "parallel")),
        name="gla_post",
    )(o_f, o_b, z, norm_g.reshape(1, GLA_DV))


def _pad_ab_columns(w):
    parts, col = [], 0
    for o_start, width, n_start in sorted(Z_SEGS.values(), key=lambda seg: seg[2]):
        if n_start > col:
            parts.append(jnp.zeros((w.shape[0], n_start - col), w.dtype))
        parts.append(w[:, o_start:o_start + width])
        col = n_start + width
    parts.append(jnp.zeros((w.shape[0], Z_COLS - col), w.dtype))
    return jnp.concatenate(parts, axis=1)


def _pad_rows(w, rows):
    pad = [(0, 0)] * w.ndim
    pad[-2] = (0, rows - w.shape[-2])
    return jnp.pad(w, pad)


def _rope_tables(T):
    inv = ROPE_THETA ** (-jnp.arange(0, DIFF_HEAD_DIM, 2, dtype=F32) / DIFF_HEAD_DIM)
    ang = jnp.arange(T, dtype=F32)[:, None] * inv[None, :]
    cos, sin = jnp.cos(ang), jnp.sin(ang)
    zero = jnp.zeros_like(sin)
    cos_t = jnp.tile(cos, (1, 4))
    sin_up = jnp.concatenate([-sin, zero, -sin, zero], axis=1)
    sin_dn = jnp.concatenate([zero, sin, zero, sin], axis=1)
    return cos_t, sin_up, sin_dn


def _lambda_init(layer):
    return 0.8 - 0.6 * math.exp(-0.3 * layer)


def _mixer_ab(x, shift, scale, ab_w_in, rwkv_mu, rwkv_w0, rwkv_w_up, rwkv_a0, rwkv_a_up,
              rwkv_g_up, rwkv_k_k, rwkv_k_a, rwkv_r_k, rwkv_gn_g, rwkv_gn_b,
              gla_gate_up, gla_gate_b, gla_norm_g, blocks):
    w_p = _pad_ab_columns(ab_w_in.astype(BF16))
    z = modulated_projection(x, shift, scale, w_p, BF16, blocks["proj_tm"], 1024)

    mu_p = jnp.zeros((1, Z_A_COLS), F32)
    for name in ("rkv", "wd", "ad", "gd"):
        o_start, width, n_start = Z_SEGS[name]
        mu_p = lax.dynamic_update_slice(mu_p, rwkv_mu[None, o_start:o_start + width], (0, n_start))
    r, k, v, kk, ba, lw, g = rwkv_pre(
        z, mu_p, rwkv_w0, _pad_rows(rwkv_w_up, LORA_PAD), rwkv_a0, _pad_rows(rwkv_a_up, LORA_PAD),
        rwkv_g_up, rwkv_k_k, rwkv_k_a, blocks["pre_tm"])
    y_f, y_r = rwkv_scan(r, k, v, kk, ba, lw, blocks["rwkv_tl"])
    y_a = rwkv_post(y_f, y_r, r, k, v, g, rwkv_r_k, rwkv_gn_g, rwkv_gn_b, blocks["pre_tm"])

    o_f, o_r = gla_scan(z, _pad_rows(gla_gate_up, LANES), gla_gate_b, blocks["scan_tl"])
    y_b = gla_post(o_f, o_r, z, gla_norm_g, blocks["post_tm"])
    return y_a, y_b


def _mixer_c(x, shift, scale, diff_w_in, diff_lambda, diff_subln_g, lambda_init, blocks):
    T = x.shape[1]
    qkv = qkv_projection(x, shift, scale, diff_w_in.astype(BF16), _rope_tables(T),
                         blocks["proj_tm"], 1024)
    return diff_attention(qkv, diff_lambda, diff_subln_g, lambda_init, blocks["attn_tq"])


def _blocks(T):
    return {
        "proj_tm": min(1024, T),
        "pre_tm": min(256, T),
        "post_tm": min(1024, T),
        "rwkv_tl": min(1024, T),
        "scan_tl": min(1024, T),
        "attn_tq": min(256, T),
        "ln_tm": min(512, T),
        "ffn_tm": min(512, T),
    }


def kernel(x, c, ada_w, ada_b, ln_g, ln_b, ffn_w_in, ffn_w_out, ab_w_in, ab_w_out, rwkv_mu, rwkv_w0, rwkv_w_up, rwkv_a0, rwkv_a_up, rwkv_g_up, rwkv_k_k, rwkv_k_a, rwkv_r_k, rwkv_gn_g, rwkv_gn_b, gla_gate_up, gla_gate_b, gla_norm_g, diff_w_in, diff_w_out, diff_lambda, diff_subln_g):
    blocks = _blocks(x.shape[1])
    shift, scale, gate = ada_modulation(c, ada_w, ada_b)
    ffn_w_in_b = ffn_w_in.astype(BF16)
    ffn_w_out_b = ffn_w_out.astype(BF16)
    for i in range(DEPTH):
        j = i // 2
        m = 2 * i
        if i % 2 == 0:
            y_parts = _mixer_ab(
                x, shift[m], scale[m], ab_w_in[j], rwkv_mu[j], rwkv_w0[j], rwkv_w_up[j],
                rwkv_a0[j], rwkv_a_up[j], rwkv_g_up[j], rwkv_k_k[j], rwkv_k_a[j], rwkv_r_k[j],
                rwkv_gn_g[j], rwkv_gn_b[j], gla_gate_up[j], gla_gate_b[j], gla_norm_g[j], blocks)
            w_out = ab_w_out[j]
        else:
            y_parts = (_mixer_c(x, shift[m], scale[m], diff_w_in[j], diff_lambda[j],
                                diff_subln_g[j], _lambda_init(i), blocks),)
            w_out = diff_w_out[j]
        x = projection_layernorm(y_parts, w_out.astype(BF16), x, gate[m], ln_g[i, 0], ln_b[i, 0],
                                 blocks["ln_tm"])
        x = ffn_sublayer(x, shift[m + 1], scale[m + 1], gate[m + 1], ffn_w_in_b, ffn_w_out_b, i,
                         ln_g[i, 1], ln_b[i, 1], blocks["ffn_tm"], 512)
    return x
```
